```python
import jax, jax.numpy as jnp
from jax import lax
import numpy as np

D_MODEL = 2048
BATCH = 4
SEQ = 2048
DEPTH = 4
DEC_BATCH = 128
DEC_SEQ = 8
PAST_LEN = 8192
PAGE_SIZE = 128

N_MEM = 256
MIX_W = D_MODEL
CONV_CH = D_MODEL // 2
NOPE_DIM = 128
ROPE_DIM = 64
V_DIM = 128
N_HEADS = (MIX_W - CONV_CH) // V_DIM
Q_LORA = D_MODEL // 4
KV_LORA = D_MODEL // 8
CONV_K = 31
X_HEADS = 4
X_HEAD_DIM = 128
FFN_DIM = -(-8 * D_MODEL // (3 * 256)) * 256
IN_COLS = 2 * CONV_CH + Q_LORA + KV_LORA + ROPE_DIM
SPLITS = (CONV_CH, 2 * CONV_CH, 2 * CONV_CH + Q_LORA, 2 * CONV_CH + Q_LORA + KV_LORA)
Q_BLOCK = 128
EPS = 1e-6
ROPE_THETA = 10000.0
MLA_SCALE = (NOPE_DIM + ROPE_DIM) ** -0.5

kernel_name = 'hymba_conv_mla_memxattn_decoder_step'


def rmsnorm(x, g):
    xf = x.astype(jnp.float32)
    xf = xf * lax.rsqrt(jnp.mean(jnp.square(xf), axis=-1, keepdims=True) + EPS)
    return xf.astype(x.dtype) * g


def layernorm(x, g, b):
    xf = x.astype(jnp.float32)
    mu = jnp.mean(xf, axis=-1, keepdims=True)
    var = jnp.mean(jnp.square(xf - mu), axis=-1, keepdims=True)
    return ((xf - mu) * lax.rsqrt(var + EPS)).astype(x.dtype) * g + b


def rope(x, pos):
    half = ROPE_DIM // 2
    inv = jnp.power(ROPE_THETA, -jnp.arange(half, dtype=jnp.float32) / half)
    ang = pos.astype(jnp.float32)[:, None] * inv[None, :]
    cos = jnp.cos(ang)[None, :, None, :]
    sin = jnp.sin(ang)[None, :, None, :]
    x1 = x[..., :half].astype(jnp.float32)
    x2 = x[..., half:].astype(jnp.float32)
    return jnp.concatenate([x1 * cos - x2 * sin, x1 * sin + x2 * cos], axis=-1).astype(x.dtype)


def causal_depthwise_conv(u_hist, w, b):
    y = lax.conv_general_dilated(u_hist, w[:, None, :], window_strides=(1,), padding='VALID',
                                 dimension_numbers=('NWC', 'WIO', 'NWC'),
                                 feature_group_count=u_hist.shape[-1])
    return y + b


def mla_attend(q_lat, q_rope, ckv, kr, q_pos, k_pos):
    s = jnp.einsum('bqhc,bkc->bhqk', q_lat, ckv) + jnp.einsum('bqhr,bkr->bhqk', q_rope, kr)
    s = s.astype(jnp.float32) * MLA_SCALE
    mask = k_pos[None, :] <= q_pos[:, None]
    p = jax.nn.softmax(jnp.where(mask, s, -jnp.inf), axis=-1).astype(ckv.dtype)
    return jnp.einsum('bhqk,bkc->bqhc', p, ckv)


def mixer_sublayer(h, pos, conv_prefix, ckv_past, kr_past, w_in, q_norm_g, w_uq, kv_norm_g,
                   w_uk, w_uv, conv_w, conv_b, conv_ln_g, conv_ln_b, w_o):
    B, T, _ = h.shape
    z = h @ w_in
    u_val, u_gate, cq, ckv, kr = jnp.split(z, SPLITS, axis=-1)
    u = u_val * jax.nn.sigmoid(u_gate)
    u_hist = jnp.concatenate([conv_prefix, u], axis=1)
    a = causal_depthwise_conv(u_hist, conv_w, conv_b)
    a = jax.nn.silu(layernorm(a, conv_ln_g, conv_ln_b))
    new_conv = u_hist[:, -(CONV_K - 1):]
    q = (rmsnorm(cq, q_norm_g) @ w_uq).reshape(B, T, N_HEADS, NOPE_DIM + ROPE_DIM)
    q_rope = rope(q[..., NOPE_DIM:], pos)
    q_lat = jnp.einsum('bthd,chd->bthc', q[..., :NOPE_DIM], w_uk)
    ckv = rmsnorm(ckv, kv_norm_g)
    kr = rope(kr[:, :, None, :], pos)[:, :, 0, :]
    if ckv_past is None:
        nb = T // Q_BLOCK
        to_blocks = lambda arr: jnp.moveaxis(arr.reshape(B, nb, Q_BLOCK, *arr.shape[2:]), 1, 0)
        o_lat = lax.map(lambda args: mla_attend(args[0], args[1], ckv, kr, args[2], pos),
                        (to_blocks(q_lat), to_blocks(q_rope), pos.reshape(nb, Q_BLOCK)))
        o_lat = jnp.moveaxis(o_lat, 0, 1).reshape(B, T, N_HEADS, KV_LORA)
    else:
        keys_c = jnp.concatenate([ckv_past, ckv], axis=1)
        keys_r = jnp.concatenate([kr_past, kr], axis=1)
        k_pos = jnp.arange(keys_c.shape[1], dtype=jnp.int32)
        o_lat = mla_attend(q_lat, q_rope, keys_c, keys_r, pos, k_pos)
    o = jnp.einsum('bthc,chd->bthd', o_lat, w_uv).reshape(B, T, N_HEADS * V_DIM)
    y = jnp.concatenate([a, o], axis=-1) @ w_o
    return y, new_conv, ckv, kr


def cross_attend(h, mem_k, mem_v, xq_w, xo_w):
    B, T, _ = h.shape
    q = (h @ xq_w).reshape(B, T, X_HEADS, X_HEAD_DIM)
    s = jnp.einsum('bthd,bmhd->bhtm', q, mem_k).astype(jnp.float32) * (X_HEAD_DIM ** -0.5)
    p = jax.nn.softmax(s, axis=-1).astype(h.dtype)
    o = jnp.einsum('bhtm,bmhd->bthd', p, mem_v).reshape(B, T, X_HEADS * X_HEAD_DIM)
    return o @ xo_w


def swiglu(h, w_gate, w_up, w_down):
    return (jax.nn.silu(h @ w_gate) * (h @ w_up)) @ w_down


def setup_inputs(seed: int = 0) -> dict:
    key = jax.random.key(seed)
    ks = iter(jax.random.split(key, 48))
    f32 = jnp.float32

    def nrm(shape, scale=1.0):
        return jax.random.normal(next(ks), shape, f32) * scale

    def gain(shape):
        return 1.0 + 0.02 * nrm(shape)

    n_pages = PAST_LEN // PAGE_SIZE
    n_used = DEC_BATCH * n_pages
    n_phys = n_used + n_used // 4
    perm = jax.random.permutation(next(ks), n_phys)
    page_table = perm[:n_used].reshape(DEC_BATCH, n_pages).astype(jnp.int32)
    L = DEPTH
    return {
        'x_prompt': nrm((BATCH, SEQ, D_MODEL)),
        'x_sample': nrm((DEC_BATCH, DEC_SEQ, D_MODEL)),
        'mem_prompt': nrm((BATCH, N_MEM, D_MODEL)),
        'cache_kv_latent': nrm((L, n_phys, PAGE_SIZE, KV_LORA)),
        'cache_k_rope': nrm((L, n_phys, PAGE_SIZE, ROPE_DIM)),
        'cache_mem_k': nrm((L, DEC_BATCH, N_MEM, X_HEADS, X_HEAD_DIM)),
        'cache_mem_v': nrm((L, DEC_BATCH, N_MEM, X_HEADS, X_HEAD_DIM)),
        'state_conv': nrm((L, DEC_BATCH, CONV_K - 1, CONV_CH), 0.5),
        'page_table': page_table,
        'ln_mix_g': gain((L, D_MODEL)),
        'w_in': nrm((L, D_MODEL, IN_COLS), D_MODEL ** -0.5),
        'q_norm_g': gain((L, Q_LORA)),
        'w_uq': nrm((L, Q_LORA, N_HEADS * (NOPE_DIM + ROPE_DIM)), Q_LORA ** -0.5),
        'kv_norm_g': gain((L, KV_LORA)),
        'w_uk': nrm((L, KV_LORA, N_HEADS, NOPE_DIM), KV_LORA ** -0.5),
        'w_uv': nrm((L, KV_LORA, N_HEADS, V_DIM), KV_LORA ** -0.5),
        'conv_w': nrm((L, CONV_K, CONV_CH), CONV_K ** -0.5),
        'conv_b': nrm((L, CONV_CH), 0.02),
        'conv_ln_g': gain((L, CONV_CH)),
        'conv_ln_b': nrm((L, CONV_CH), 0.02),
        'w_o': nrm((L, MIX_W, D_MODEL), MIX_W ** -0.5),
        'ln_x_g': gain((L, D_MODEL)),
        'ln_mem_g': gain((L, D_MODEL)),
        'xq_w': nrm((L, D_MODEL, X_HEADS * X_HEAD_DIM), D_MODEL ** -0.5),
        'xk_w': nrm((L, D_MODEL, X_HEADS * X_HEAD_DIM), D_MODEL ** -0.5),
        'xv_w': nrm((L, D_MODEL, X_HEADS * X_HEAD_DIM), D_MODEL ** -0.5),
        'xo_w': nrm((L, X_HEADS * X_HEAD_DIM, D_MODEL), (X_HEADS * X_HEAD_DIM) ** -0.5),
        'ln_ffn_g': gain((L, D_MODEL)),
        'ffn_w_gate': nrm((L, D_MODEL, FFN_DIM), D_MODEL ** -0.5),
        'ffn_w_up': nrm((L, D_MODEL, FFN_DIM), D_MODEL ** -0.5),
        'ffn_w_down': nrm((L, FFN_DIM, D_MODEL), FFN_DIM ** -0.5),
        'ln_f_g': gain((D_MODEL,)),
    }


def reference(x_prompt, x_sample, mem_prompt, cache_kv_latent, cache_k_rope, cache_mem_k,
              cache_mem_v, state_conv, page_table, ln_mix_g, w_in, q_norm_g, w_uq, kv_norm_g,
              w_uk, w_uv, conv_w, conv_b, conv_ln_g, conv_ln_b, w_o, ln_x_g, ln_mem_g, xq_w,
              xk_w, xv_w, xo_w, ln_ffn_g, ffn_w_gate, ffn_w_up, ffn_w_down, ln_f_g):
    Bp, S, _ = x_prompt.shape
    Bd, T, _ = x_sample.shape
    n_mem = mem_prompt.shape[1]
    past_len = page_table.shape[1] * PAGE_SIZE
    pos_p = jnp.arange(S, dtype=jnp.int32)
    pos_s = past_len + jnp.arange(T, dtype=jnp.int32)
    conv_zero = jnp.zeros((Bp, CONV_K - 1, CONV_CH), x_prompt.dtype)
    xp, xs = x_prompt, x_sample
    conv_p, kvl_p, kr_p, mk_p, mv_p = [], [], [], [], []
    conv_s, kvl_s, kr_s = [], [], []
    for l in range(DEPTH):
        mix_w = (w_in[l], q_norm_g[l], w_uq[l], kv_norm_g[l], w_uk[l], w_uv[l],
                 conv_w[l], conv_b[l], conv_ln_g[l], conv_ln_b[l], w_o[l])
        yp, cp, kvp, krp = mixer_sublayer(rmsnorm(xp, ln_mix_g[l]), pos_p, conv_zero, None, None, *mix_w)
        xp = xp + yp
        ckv_past = cache_kv_latent[l, page_table].reshape(Bd, past_len, KV_LORA)
        kr_past = cache_k_rope[l, page_table].reshape(Bd, past_len, ROPE_DIM)
        ys, cs, kvs, krs = mixer_sublayer(rmsnorm(xs, ln_mix_g[l]), pos_s, state_conv[l], ckv_past, kr_past, *mix_w)
        xs = xs + ys
        mem_h = rmsnorm(mem_prompt, ln_mem_g[l])
        mk = (mem_h @ xk_w[l]).reshape(Bp, n_mem, X_HEADS, X_HEAD_DIM)
        mv = (mem_h @ xv_w[l]).reshape(Bp, n_mem, X_HEADS, X_HEAD_DIM)
        xp = xp + cross_attend(rmsnorm(xp, ln_x_g[l]), mk, mv, xq_w[l], xo_w[l])
        xs = xs + cross_attend(rmsnorm(xs, ln_x_g[l]), cache_mem_k[l], cache_mem_v[l], xq_w[l], xo_w[l])
        xp = xp + swiglu(rmsnorm(xp, ln_ffn_g[l]), ffn_w_gate[l], ffn_w_up[l], ffn_w_down[l])
        xs = xs + swiglu(rmsnorm(xs, ln_ffn_g[l]), ffn_w_gate[l], ffn_w_up[l], ffn_w_down[l])
        conv_p.append(cp); kvl_p.append(kvp); kr_p.append(krp); mk_p.append(mk); mv_p.append(mv)
        conv_s.append(cs); kvl_s.append(kvs); kr_s.append(krs)
    y_prompt = rmsnorm(xp, ln_f_g)
    y_sample = rmsnorm(xs, ln_f_g)
    return (y_prompt, y_sample,
            jnp.stack(conv_p), jnp.stack(kvl_p), jnp.stack(kr_p), jnp.stack(mk_p), jnp.stack(mv_p),
            jnp.stack(conv_s), jnp.stack(kvl_s), jnp.stack(kr_s))
```

```python
import functools
import math

import jax
import jax.numpy as jnp
from jax import lax
from jax.experimental import pallas as pl
from jax.experimental.pallas import tpu as pltpu

F32 = jnp.float32
BF16 = jnp.bfloat16
EPS = 1e-6
ROPE_THETA = 10000.0
VMEM_LIMIT = 56 * 1024 * 1024
NEG = -1e30


def _dot(a, b):
    return jnp.dot(a, b, preferred_element_type=F32)


def _dot_nt(a, b):
    return lax.dot_general(a, b, (((1,), (1,)), ((), ())), preferred_element_type=F32)


def _rms(x, g):
    ms = jnp.mean(x * x, axis=-1, keepdims=True)
    return (x * lax.rsqrt(ms + EPS)) * g


def _tile(n, pref, mult=8):
    t = min(n, pref)
    while t > mult and (n % t or t % mult):
        t -= 1
    assert n % t == 0 and (t % mult == 0 or t == n), (n, pref, mult)
    return t


def _params(*sem):
    return pltpu.CompilerParams(dimension_semantics=sem, vmem_limit_bytes=VMEM_LIMIT)


def _full(shape):
    zeros = (0,) * len(shape)
    return pl.BlockSpec(shape, lambda *_: zeros)


def _layer(shape, l):
    zeros = (0,) * len(shape)
    return pl.BlockSpec((None,) + tuple(shape), lambda *_: (l,) + zeros)


def _memkv_kernel(m_ref, g_ref, wk_ref, wv_ref, k_ref, v_ref):
    h = _rms(m_ref[...], g_ref[...]).astype(BF16)
    k_ref[...] = _dot(h, wk_ref[...])
    v_ref[...] = _dot(h, wv_ref[...])


def _mem_kv(mem, g, wk, wv):
    Mm, D = mem.shape
    L, _, E = wk.shape
    tm = _tile(Mm, 512)
    return pl.pallas_call(
        _memkv_kernel,
        grid=(L, Mm // tm),
        in_specs=[pl.BlockSpec((tm, D), lambda l, i: (i, 0)),
                  pl.BlockSpec((None, 1, D), lambda l, i: (l, 0, 0)),
                  pl.BlockSpec((None, D, E), lambda l, i: (l, 0, 0)),
                  pl.BlockSpec((None, D, E), lambda l, i: (l, 0, 0))],
        out_specs=[pl.BlockSpec((None, tm, E), lambda l, i: (l, i, 0))] * 2,
        out_shape=[jax.ShapeDtypeStruct((L, Mm, E), F32)] * 2,
        compiler_params=_params("parallel", "parallel"),
        name="mem_kv",
    )(mem, g, wk, wv)


def _inproj_kernel(x_ref, g_ref, wval_ref, wgate_ref, wcq_ref, wckv_ref, wkr_ref, wkrs_ref,
                   qg_ref, kvg_ref, wqn_ref, wqr_ref, wqrs_ref, wuk_ref, cos_ref, sin_ref,
                   u_ref, ckv_ref, kr_ref, ckvb_ref, krb_ref, qlat_ref, qrope_ref, *, scale):
    n_heads, nope, _ = wuk_ref.shape
    rope = cos_ref.shape[-1]
    h = _rms(x_ref[...], g_ref[...]).astype(BF16)
    u_ref[...] = _dot(h, wval_ref[...]) * jax.nn.sigmoid(_dot(h, wgate_ref[...]))
    cqn = _rms(_dot(h, wcq_ref[...]), qg_ref[...]).astype(BF16)
    ckv = _rms(_dot(h, wckv_ref[...]), kvg_ref[...])
    ckv_ref[...] = ckv
    ckvb_ref[...] = ckv.astype(BF16)
    cos = cos_ref[...]
    sin = sin_ref[...]
    kr = _dot(h, wkr_ref[...]) * cos + _dot(h, wkrs_ref[...]) * sin
    kr_ref[...] = kr
    krb_ref[...] = kr.astype(BF16)
    qn = _dot(cqn, wqn_ref[...])
    qr = _dot(cqn, wqr_ref[...])
    qrs = _dot(cqn, wqrs_ref[...])
    for hd in range(n_heads):
        qn_h = qn[:, hd * nope:(hd + 1) * nope].astype(BF16)
        qlat_ref[hd] = (_dot(qn_h, wuk_ref[hd]) * scale).astype(BF16)
        sl = slice(hd * rope, (hd + 1) * rope)
        qrope_ref[hd] = ((qr[:, sl] * cos + qrs[:, sl] * sin) * scale).astype(BF16)


def _in_proj(x, w, l, cos, sin, scale):
    M, D = x.shape
    tm = _tile(M, 512, 16)
    n_heads, nope, kvl = w["w_ukt"].shape[1:]
    conv_ch = w["w_val"].shape[-1]
    ql = w["w_cq"].shape[-1]
    rope = cos.shape[-1]
    row = lambda c: pl.BlockSpec((tm, c), lambda i: (i, 0))
    heads = lambda c: pl.BlockSpec((n_heads, tm, c), lambda i: (0, i, 0))
    return pl.pallas_call(
        functools.partial(_inproj_kernel, scale=scale),
        grid=(M // tm,),
        in_specs=[row(D), _layer((1, D), l),
                  _layer((D, conv_ch), l), _layer((D, conv_ch), l), _layer((D, ql), l),
                  _layer((D, kvl), l), _layer((D, rope), l), _layer((D, rope), l),
                  _layer((1, ql), l), _layer((1, kvl), l),
                  _layer((ql, n_heads * nope), l), _layer((ql, n_heads * rope), l),
                  _layer((ql, n_heads * rope), l), _layer((n_heads, nope, kvl), l),
                  row(rope), row(rope)],
        out_specs=[row(conv_ch), row(kvl), row(rope), row(kvl), row(rope), heads(kvl), heads(rope)],
        out_shape=[jax.ShapeDtypeStruct((M, conv_ch), F32),
                   jax.ShapeDtypeStruct((M, kvl), F32),
                   jax.ShapeDtypeStruct((M, rope), F32),
                   jax.ShapeDtypeStruct((M, kvl), BF16),
                   jax.ShapeDtypeStruct((M, rope), BF16),
                   jax.ShapeDtypeStruct((n_heads, M, kvl), BF16),
                   jax.ShapeDtypeStruct((n_heads, M, rope), BF16)],
        compiler_params=_params("parallel"),
        name="in_proj",
    )(x, w["ln_mix_g"], w["w_val"], w["w_gate"], w["w_cq"], w["w_ckv"], w["w_kr"], w["w_kr_sw"],
      w["q_norm_g"], w["kv_norm_g"], w["w_qn"], w["w_qr"], w["w_qr_sw"], w["w_ukt"], cos, sin)


CONV_ROWS = 64
LANES = 128


def _ln_swish(y, g, b):
    mu = jnp.mean(y, axis=-1, keepdims=True)
    d = y - mu
    var = jnp.mean(d * d, axis=-1, keepdims=True)
    z = d * lax.rsqrt(var + EPS) * g + b
    return z * jax.nn.sigmoid(z)


def _conv_prompt_kernel(u_ref, prev_ref, w_ref, b_ref, g_ref, lb_ref, a_ref, tail_ref, hist_ref, y_ref,
                        *, halo):
    t = pl.program_id(1)
    tt, ch = u_ref.shape
    K = w_ref.shape[0]
    off = halo - (K - 1)

    @pl.when(t == 0)
    def _():
        hist_ref[0:halo, :] = jnp.zeros((halo, ch), F32)

    @pl.when(t > 0)
    def _():
        hist_ref[0:halo, :] = prev_ref[...]

    hist_ref[halo:halo + tt, :] = u_ref[...]
    for r0 in range(0, tt, CONV_ROWS):
        for c0 in range(0, ch, LANES):
            cs = slice(c0, c0 + LANES)
            acc = jnp.zeros((CONV_ROWS, LANES), F32)
            for k in range(K):
                acc = acc + w_ref[k:k + 1, cs] * hist_ref[off + r0 + k:off + r0 + k + CONV_ROWS, cs]
            y_ref[r0:r0 + CONV_ROWS, cs] = acc
    y = y_ref[...] + b_ref[...]
    a_ref[...] = _ln_swish(y, g_ref[...], lb_ref[...]).astype(BF16)
    tail_ref[...] = hist_ref[halo + tt - (K - 1):halo + tt, :]


def _conv_prompt(u, w, l, batch, seq):
    ch = u.shape[-1]
    K = w["conv_w"].shape[1]
    halo = 32
    assert K - 1 <= halo
    tt = _tile(seq, 128, CONV_ROWS)
    nt = seq // tt
    assert tt % halo == 0 and tt % CONV_ROWS == 0 and ch % LANES == 0
    hb = tt // halo
    return pl.pallas_call(
        functools.partial(_conv_prompt_kernel, halo=halo),
        grid=(batch, nt),
        in_specs=[pl.BlockSpec((tt, ch), lambda b, t: (b * nt + t, 0)),
                  pl.BlockSpec((halo, ch), lambda b, t: (jnp.maximum((b * nt + t) * hb - 1, 0), 0)),
                  _layer((K, ch), l), _layer((1, ch), l), _layer((1, ch), l), _layer((1, ch), l)],
        out_specs=[pl.BlockSpec((tt, ch), lambda b, t: (b * nt + t, 0)),
                   pl.BlockSpec((None, K - 1, ch), lambda b, t: (b, 0, 0))],
        out_shape=[jax.ShapeDtypeStruct((batch * seq, ch), BF16),
                   jax.ShapeDtypeStruct((batch, K - 1, ch), F32)],
        scratch_shapes=[pltpu.VMEM((halo + tt, ch), F32), pltpu.VMEM((tt, ch), F32)],
        compiler_params=_params("parallel", "arbitrary"),
        name="conv_prompt",
    )(u, u, w["conv_w"], w["conv_b"], w["conv_ln_g"], w["conv_ln_b"])


def _conv_sample_kernel(st_ref, u_ref, w_ref, b_ref, g_ref, lb_ref, a_ref, ns_ref, hist_ref):
    bb, T, ch = u_ref.shape
    K = w_ref.shape[0]
    hist_ref[:, 0:K - 1, :] = st_ref[...]
    hist_ref[:, K - 1:K - 1 + T, :] = u_ref[...]
    for c0 in range(0, ch, LANES):
        cs = slice(c0, c0 + LANES)
        acc = jnp.zeros((bb, T, LANES), F32)
        for k in range(K):
            acc = acc + w_ref[k:k + 1, cs][None] * hist_ref[:, k:k + T, cs]
        y = acc + b_ref[:, cs][None]
        a_ref[:, :, cs] = y
    y = a_ref[...]
    a_ref[...] = _ln_swish(y, g_ref[...][None], lb_ref[...][None])
    ns_ref[...] = hist_ref[:, T:T + K - 1, :]


def _conv_sample(state, u_s, w, l):
    Bd, T, ch = u_s.shape
    K = w["conv_w"].shape[1]
    bb = _tile(Bd, 8, 1)
    hist_rows = -(-(K - 1 + T) // 8) * 8
    return pl.pallas_call(
        _conv_sample_kernel,
        grid=(Bd // bb,),
        in_specs=[pl.BlockSpec((None, bb, K - 1, ch), lambda i: (l, i, 0, 0)),
                  pl.BlockSpec((bb, T, ch), lambda i: (i, 0, 0)),
                  _layer((K, ch), l), _layer((1, ch), l), _layer((1, ch), l), _layer((1, ch), l)],
        out_specs=[pl.BlockSpec((bb, T, ch), lambda i: (i, 0, 0)),
                   pl.BlockSpec((bb, K - 1, ch), lambda i: (i, 0, 0))],
        out_shape=[jax.ShapeDtypeStruct((Bd, T, ch), F32),
                   jax.ShapeDtypeStruct((Bd, K - 1, ch), F32)],
        scratch_shapes=[pltpu.VMEM((bb, hist_rows, ch), F32)],
        compiler_params=_params("parallel"),
        name="conv_sample",
    )(state, u_s, w["conv_w"], w["conv_b"], w["conv_ln_g"], w["conv_ln_b"])


def _attn_prompt_kernel(ql_ref, qr_ref, k_ref, r_ref, o_ref, m_ref, l_ref, acc_ref):
    qi = pl.program_id(1)
    n_heads, tq, kvl = ql_ref.shape
    rows = n_heads * tq
    ql = ql_ref[...].reshape(rows, kvl)
    qr = qr_ref[...].reshape(rows, qr_ref.shape[-1])
    m_ref[...] = jnp.full((rows, 1), NEG, F32)
    l_ref[...] = jnp.zeros((rows, 1), F32)
    acc_ref[...] = jnp.zeros((rows, kvl), F32)
    q_tok = qi * tq + lax.rem(lax.broadcasted_iota(jnp.int32, (rows, tq), 0), tq)
    k_off = lax.broadcasted_iota(jnp.int32, (rows, tq), 1)

    def chunk(c, carry):
        ks = pl.ds(pl.multiple_of(c * tq, tq), tq)
        kc = k_ref[ks, :]
        s = _dot_nt(ql, kc) + _dot_nt(qr, r_ref[ks, :])
        s = jnp.where(c * tq + k_off <= q_tok, s, NEG)
        m_old = m_ref[...]
        m_new = jnp.maximum(m_old, jnp.max(s, axis=-1, keepdims=True))
        alpha = jnp.exp(m_old - m_new)
        p = jnp.exp(s - m_new)
        l_ref[...] = alpha * l_ref[...] + jnp.sum(p, axis=-1, keepdims=True)
        acc_ref[...] = alpha * acc_ref[...] + _dot(p.astype(BF16), kc)
        m_ref[...] = m_new
        return carry

    lax.fori_loop(0, qi + 1, chunk, 0)
    o_ref[...] = (acc_ref[...] / l_ref[...]).astype(BF16).reshape(n_heads, tq, kvl)


def _attn_prompt(qlat, qrope, ckvb, krb, batch, seq):
    n_heads, _, kvl = qlat.shape
    rope = qrope.shape[-1]
    tq = _tile(seq, 256, 16)
    nq = seq // tq
    rows = n_heads * tq
    return pl.pallas_call(
        _attn_prompt_kernel,
        grid=(batch, nq),
        in_specs=[pl.BlockSpec((n_heads, tq, kvl), lambda b, q: (0, b * nq + q, 0)),
                  pl.BlockSpec((n_heads, tq, rope), lambda b, q: (0, b * nq + q, 0)),
                  pl.BlockSpec((seq, kvl), lambda b, q: (b, 0)),
                  pl.BlockSpec((seq, rope), lambda b, q: (b, 0))],
        out_specs=pl.BlockSpec((n_heads, tq, kvl), lambda b, q: (0, b * nq + q, 0)),
        out_shape=jax.ShapeDtypeStruct((n_heads, batch * seq, kvl), BF16),
        scratch_shapes=[pltpu.VMEM((rows, 1), F32), pltpu.VMEM((rows, 1), F32),
                        pltpu.VMEM((rows, kvl), F32)],
        compiler_params=_params("parallel", "arbitrary"),
        name="attn_prompt",
    )(qlat, qrope, ckvb, krb)


def _attn_sample_kernel(pt_ref, ql_ref, qr_ref, kn_ref, rn_ref, ckv_hbm, kr_hbm, o_ref,
                        kbuf, rbuf, kbf, s_ref, sem, *, layer, chunk_pages):
    b = pl.program_id(0)
    nb = pl.num_programs(0)
    n_pages, page, kvl = kbuf.shape[1:]
    rope = rbuf.shape[-1]
    n_heads, T, _ = ql_ref.shape
    rows = n_heads * T
    slot = lax.rem(b, 2)

    def page_copies(bb, sl, p):
        pg = pt_ref[bb, p]
        return (pltpu.make_async_copy(ckv_hbm.at[layer, pg], kbuf.at[sl, p], sem.at[0, sl]),
                pltpu.make_async_copy(kr_hbm.at[layer, pg], rbuf.at[sl, p], sem.at[1, sl]))

    def start_fetch(bb, sl):
        for p in range(n_pages):
            for cp in page_copies(bb, sl, p):
                cp.start()

    @pl.when(b == 0)
    def _():
        start_fetch(b, slot)

    @pl.when(b + 1 < nb)
    def _():
        start_fetch(b + 1, 1 - slot)

    ql = ql_ref[...].reshape(rows, kvl).astype(BF16)
    qr = qr_ref[...].reshape(rows, rope).astype(BF16)
    kn = kn_ref[...].astype(BF16)
    s_new = _dot_nt(ql, kn) + _dot_nt(qr, rn_ref[...].astype(BF16))
    q_tok = lax.rem(lax.broadcasted_iota(jnp.int32, (rows, T), 0), T)
    s_new = jnp.where(lax.broadcasted_iota(jnp.int32, (rows, T), 1) <= q_tok, s_new, NEG)

    for p in range(n_pages):
        for cp in page_copies(b, slot, p):
            cp.wait()

    ck = chunk_pages * page
    for c in range(n_pages // chunk_pages):
        ps = slice(c * chunk_pages, (c + 1) * chunk_pages)
        ks = slice(c * ck, (c + 1) * ck)
        kc = kbuf[slot, ps].reshape(ck, kvl).astype(BF16)
        rc = rbuf[slot, ps].reshape(ck, rope).astype(BF16)
        kbf[ks, :] = kc
        s_ref[:, ks] = _dot_nt(ql, kc) + _dot_nt(qr, rc)
    s = s_ref[...]
    m = jnp.maximum(jnp.max(s, axis=-1, keepdims=True), jnp.max(s_new, axis=-1, keepdims=True))
    p_new = jnp.exp(s_new - m)
    p = jnp.exp(s - m)
    denom = jnp.sum(p, axis=-1, keepdims=True) + jnp.sum(p_new, axis=-1, keepdims=True)
    acc = _dot(p_new.astype(BF16), kn) + _dot(p.astype(BF16), kbf[...])
    o_ref[...] = (acc / denom).reshape(n_heads, T, kvl)


def _attn_sample(page_table, qlat, qrope, ckv, kr, cache_kv, cache_kr, l, T):
    Bd, n_pages = page_table.shape
    n_heads, _, kvl = qlat.shape
    rope = qrope.shape[-1]
    page = cache_kv.shape[2]
    chunk_pages = _tile(n_pages, 16, 1)
    rows = n_heads * T
    kernel = functools.partial(_attn_sample_kernel, layer=l, chunk_pages=chunk_pages)
    return pl.pallas_call(
        kernel,
        grid_spec=pltpu.PrefetchScalarGridSpec(
            num_scalar_prefetch=1,
            grid=(Bd,),
            in_specs=[pl.BlockSpec((n_heads, T, kvl), lambda b, pt: (0, b, 0)),
                      pl.BlockSpec((n_heads, T, rope), lambda b, pt: (0, b, 0)),
                      pl.BlockSpec((T, kvl), lambda b, pt: (b, 0)),
                      pl.BlockSpec((T, rope), lambda b, pt: (b, 0)),
                      pl.BlockSpec(memory_space=pl.ANY),
                      pl.BlockSpec(memory_space=pl.ANY)],
            out_specs=pl.BlockSpec((n_heads, T, kvl), lambda b, pt: (0, b, 0)),
            scratch_shapes=[pltpu.VMEM((2, n_pages, page, kvl), F32),
                            pltpu.VMEM((2, n_pages, page, rope), F32),
                            pltpu.VMEM((n_pages * page, kvl), BF16),
                            pltpu.VMEM((rows, n_pages * page), F32),
                            pltpu.SemaphoreType.DMA((2, 2))]),
        out_shape=jax.ShapeDtypeStruct((n_heads, Bd * T, kvl), F32),
        compiler_params=_params("arbitrary"),
        name="attn_sample",
    )(page_table, qlat, qrope, ckv, kr, cache_kv, cache_kr)


def _oproj_kernel(x_ref, a_ref, ol_ref, wuv_ref, woa_ref, woo_ref, out_ref):
    n_heads = ol_ref.shape[0]
    o = jnp.concatenate([_dot(ol_ref[hd], wuv_ref[hd]).astype(BF16) for hd in range(n_heads)], axis=-1)
    out_ref[...] = x_ref[...] + _dot(a_ref[...], woa_ref[...]) + _dot(o, woo_ref[...])


def _out_proj(x, a, olat, w, l):
    M, D = x.shape
    n_heads, _, kvl = olat.shape
    vd = w["w_uvt"].shape[-1]
    ca = a.shape[-1]
    tm = _tile(M, 512, 16)
    return pl.pallas_call(
        _oproj_kernel,
        grid=(M // tm,),
        in_specs=[pl.BlockSpec((tm, D), lambda i: (i, 0)),
                  pl.BlockSpec((tm, ca), lambda i: (i, 0)),
                  pl.BlockSpec((n_heads, tm, kvl), lambda i: (0, i, 0)),
                  _layer((n_heads, kvl, vd), l), _layer((ca, D), l), _layer((n_heads * vd, D), l)],
        out_specs=pl.BlockSpec((tm, D), lambda i: (i, 0)),
        out_shape=jax.ShapeDtypeStruct((M, D), F32),
        input_output_aliases={0: 0},
        compiler_params=_params("parallel"),
        name="out_proj",
    )(x, a, olat, w["w_uvt"], w["w_o_a"], w["w_o_o"])


def _softmax_pv(s, v):
    m = jnp.max(s, axis=-1, keepdims=True)
    p = jnp.exp(s - m)
    return p, jnp.sum(p, axis=-1, keepdims=True)


def _xattn_prompt_kernel(x_ref, g_ref, wq_ref, mk_ref, mv_ref, wo_ref, out_ref, *, n_heads, scale):
    x = x_ref[...]
    h = _rms(x, g_ref[...]).astype(BF16)
    q = (_dot(h, wq_ref[...]) * scale).astype(BF16)
    hd = q.shape[-1] // n_heads
    outs = []
    for i in range(n_heads):
        cs = slice(i * hd, (i + 1) * hd)
        s = _dot_nt(q[:, cs], mk_ref[:, cs].astype(BF16))
        m = jnp.max(s, axis=-1, keepdims=True)
        p = jnp.exp(s - m)
        den = jnp.sum(p, axis=-1, keepdims=True)
        outs.append((_dot(p.astype(BF16), mv_ref[:, cs].astype(BF16)) / den).astype(BF16))
    out_ref[...] = x + _dot(jnp.concatenate(outs, axis=-1), wo_ref[...])


def _xattn_prompt(x, mk, mv, w, l, batch, seq, n_heads):
    M, D = x.shape
    E = mk.shape[-1]
    n_mem = mk.shape[1] // batch
    tm = _tile(seq, 512, 16)
    per = seq // tm
    kernel = functools.partial(_xattn_prompt_kernel, n_heads=n_heads, scale=(E // n_heads) ** -0.5)
    return pl.pallas_call(
        kernel,
        grid=(batch * per,),
        in_specs=[pl.BlockSpec((tm, D), lambda i: (i, 0)), _layer((1, D), l), _layer((D, E), l),
                  pl.BlockSpec((None, n_mem, E), lambda i: (l, i // per, 0)),
                  pl.BlockSpec((None, n_mem, E), lambda i: (l, i // per, 0)),
                  _layer((E, D), l)],
        out_specs=pl.BlockSpec((tm, D), lambda i: (i, 0)),
        out_shape=jax.ShapeDtypeStruct((M, D), F32),
        input_output_aliases={0: 0},
        compiler_params=_params("parallel"),
        name="xattn_prompt",
    )(x, w["ln_x_g"], w["xq_w"], mk, mv, w["xo_w"])


def _xattn_sample_kernel(x_ref, g_ref, wq_ref, mk_ref, mv_ref, wo_ref, out_ref, *, n_heads, scale):
    x = x_ref[...]
    bb, n_mem, E = mk_ref.shape
    T = x.shape[0] // bb
    h = _rms(x, g_ref[...]).astype(BF16)
    q = _dot(h, wq_ref[...]) * scale
    hd = E // n_heads
    outs = []
    for i in range(n_heads):
        cs = slice(i * hd, (i + 1) * hd)
        qh = q[:, cs].reshape(bb, T, hd).astype(BF16)
        kh = mk_ref[:, :, cs].astype(BF16)
        vh = mv_ref[:, :, cs].astype(BF16)
        s = jnp.einsum("btd,bmd->btm", qh, kh, preferred_element_type=F32)
        m = jnp.max(s, axis=-1, keepdims=True)
        p = jnp.exp(s - m)
        den = jnp.sum(p, axis=-1, keepdims=True)
        o = jnp.einsum("btm,bmd->btd", p.astype(BF16), vh, preferred_element_type=F32) / den
        outs.append(o.reshape(bb * T, hd).astype(BF16))
    out_ref[...] = x + _dot(jnp.concatenate(outs, axis=-1), wo_ref[...])


def _xattn_sample(x, mem_k, mem_v, w, l, row0, T, n_heads):
    M, D = x.shape
    _, Bd, n_mem, E = mem_k.shape
    bb = _tile(Bd, 8, 1)
    tm = bb * T
    assert row0 % tm == 0 and tm % 8 == 0
    blk0 = row0 // tm
    kernel = functools.partial(_xattn_sample_kernel, n_heads=n_heads, scale=(E // n_heads) ** -0.5)
    return pl.pallas_call(
        kernel,
        grid=(Bd // bb,),
        in_specs=[pl.BlockSpec((tm, D), lambda i: (blk0 + i, 0)), _layer((1, D), l), _layer((D, E), l),
                  pl.BlockSpec((None, bb, n_mem, E), lambda i: (l, i, 0, 0)),
                  pl.BlockSpec((None, bb, n_mem, E), lambda i: (l, i, 0, 0)),
                  _layer((E, D), l)],
        out_specs=pl.BlockSpec((tm, D), lambda i: (blk0 + i, 0)),
        out_shape=jax.ShapeDtypeStruct((M, D), F32),
        input_output_aliases={0: 0},
        compiler_params=_params("parallel"),
        name="xattn_sample",
    )(x, w["ln_x_g"], w["xq_w"], mem_k, mem_v, w["xo_w"])


def _ffn_kernel(x_ref, g_ref, wg_ref, wu_ref, wd_ref, out_ref, h_ref, acc_ref):
    f = pl.program_id(1)

    @pl.when(f == 0)
    def _():
        h_ref[...] = _rms(x_ref[...], g_ref[...]).astype(BF16)
        acc_ref[...] = jnp.zeros_like(acc_ref)

    h = h_ref[...]
    gate = _dot(h, wg_ref[...])
    act = (gate * jax.nn.sigmoid(gate) * _dot(h, wu_ref[...])).astype(BF16)
    acc_ref[...] += _dot(act, wd_ref[...])

    @pl.when(f == pl.num_programs(1) - 1)
    def _():
        out_ref[...] = x_ref[...] + acc_ref[...]


def _ffn(x, w, l):
    M, D = x.shape
    Fd = w["ffn_w_gate"].shape[-1]
    tm = _tile(M, 512, 16)
    tf = _tile(Fd, 512, 128)
    return pl.pallas_call(
        _ffn_kernel,
        grid=(M // tm, Fd // tf),
        in_specs=[pl.BlockSpec((tm, D), lambda i, f: (i, 0)), _layer((1, D), l),
                  pl.BlockSpec((None, D, tf), lambda i, f: (l, 0, f)),
                  pl.BlockSpec((None, D, tf), lambda i, f: (l, 0, f)),
                  pl.BlockSpec((None, tf, D), lambda i, f: (l, f, 0))],
        out_specs=pl.BlockSpec((tm, D), lambda i, f: (i, 0)),
        out_shape=jax.ShapeDtypeStruct((M, D), F32),
        scratch_shapes=[pltpu.VMEM((tm, D), BF16), pltpu.VMEM((tm, D), F32)],
        input_output_aliases={0: 0},
        compiler_params=_params("parallel", "arbitrary"),
        name="ffn",
    )(x, w["ln_ffn_g"], w["ffn_w_gate"], w["ffn_w_up"], w["ffn_w_down"])


def _norm_kernel(x_ref, g_ref, o_ref):
    o_ref[...] = _rms(x_ref[...], g_ref[...])


def _final_norm(x, g, row0, rows):
    M, D = x.shape
    tm = _tile(rows, 512)
    assert row0 % tm == 0
    blk0 = row0 // tm
    return pl.pallas_call(
        _norm_kernel,
        grid=(rows // tm,),
        in_specs=[pl.BlockSpec((tm, D), lambda i: (blk0 + i, 0)), _full((1, D))],
        out_specs=pl.BlockSpec((tm, D), lambda i: (i, 0)),
        out_shape=jax.ShapeDtypeStruct((rows, D), F32),
        compiler_params=_params("parallel"),
        name="final_norm",
    )(x, g)


def _half_swap_cols(w, width):
    shp = w.shape
    g = w.reshape(shp[:-1] + (shp[-1] // width, 2, width // 2))
    return jnp.flip(g, axis=-2).reshape(shp)


def _rope_tables(pos, rope):
    half = rope // 2
    inv = jnp.power(ROPE_THETA, -jnp.arange(half, dtype=F32) / half)
    ang = pos.astype(F32)[:, None] * inv[None, :]
    cos, sin = jnp.cos(ang), jnp.sin(ang)
    return jnp.concatenate([cos, cos], axis=-1), jnp.concatenate([-sin, sin], axis=-1)


def kernel(x_prompt, x_sample, mem_prompt, cache_kv_latent, cache_k_rope, cache_mem_k, cache_mem_v, state_conv, page_table, ln_mix_g, w_in, q_norm_g, w_uq, kv_norm_g, w_uk, w_uv, conv_w, conv_b, conv_ln_g, conv_ln_b, w_o, ln_x_g, ln_mem_g, xq_w, xk_w, xv_w, xo_w, ln_ffn_g, ffn_w_gate, ffn_w_up, ffn_w_down, ln_f_g):
    Bp, S, D = x_prompt.shape
    Bd, T, _ = x_sample.shape
    L = w_in.shape[0]
    n_mem = mem_prompt.shape[1]
    page = cache_kv_latent.shape[2]
    kvl = cache_kv_latent.shape[3]
    rope = cache_k_rope.shape[3]
    n_heads, nope = w_uk.shape[2], w_uk.shape[3]
    vd = w_uv.shape[3]
    conv_ch = conv_w.shape[2]
    K = conv_w.shape[1]
    ql = q_norm_g.shape[1]
    x_heads, x_hd = cache_mem_k.shape[3], cache_mem_k.shape[4]
    E = x_heads * x_hd
    past_len = page_table.shape[1] * page
    Mp, Ms = Bp * S, Bd * T
    scale = float(nope + rope) ** -0.5

    vec = lambda g: g.reshape(L, 1, -1)
    w_uq_h = w_uq.reshape(L, ql, n_heads, nope + rope)
    w_qr = w_uq_h[..., nope:].reshape(L, ql, n_heads * rope)
    w_kr = w_in[:, :, 2 * conv_ch + ql + kvl:]
    w = {
        "ln_mix_g": vec(ln_mix_g), "q_norm_g": vec(q_norm_g), "kv_norm_g": vec(kv_norm_g),
        "w_val": w_in[:, :, :conv_ch].astype(BF16),
        "w_gate": w_in[:, :, conv_ch:2 * conv_ch].astype(BF16),
        "w_cq": w_in[:, :, 2 * conv_ch:2 * conv_ch + ql].astype(BF16),
        "w_ckv": w_in[:, :, 2 * conv_ch + ql:2 * conv_ch + ql + kvl].astype(BF16),
        "w_kr": w_kr.astype(BF16),
        "w_kr_sw": _half_swap_cols(w_kr, rope).astype(BF16),
        "w_qn": w_uq_h[..., :nope].reshape(L, ql, n_heads * nope).astype(BF16),
        "w_qr": w_qr.astype(BF16),
        "w_qr_sw": _half_swap_cols(w_qr, rope).astype(BF16),
        "w_ukt": jnp.transpose(w_uk, (0, 2, 3, 1)).astype(BF16),
        "w_uvt": jnp.transpose(w_uv, (0, 2, 1, 3)).astype(BF16),
        "conv_w": conv_w, "conv_b": vec(conv_b), "conv_ln_g": vec(conv_ln_g), "conv_ln_b": vec(conv_ln_b),
        "w_o_a": w_o[:, :conv_ch].astype(BF16), "w_o_o": w_o[:, conv_ch:].astype(BF16),
        "ln_x_g": vec(ln_x_g), "xq_w": xq_w.astype(BF16), "xo_w": xo_w.astype(BF16),
        "ln_ffn_g": vec(ln_ffn_g), "ffn_w_gate": ffn_w_gate.astype(BF16),
        "ffn_w_up": ffn_w_up.astype(BF16), "ffn_w_down": ffn_w_down.astype(BF16),
    }
    pos = jnp.concatenate([jnp.tile(jnp.arange(S, dtype=jnp.int32), Bp),
                           jnp.tile(past_len + jnp.arange(T, dtype=jnp.int32), Bd)])
    cos, sin = _rope_tables(pos, rope)

    mk_all, mv_all = _mem_kv(mem_prompt.reshape(Bp * n_mem, D), vec(ln_mem_g),
                             xk_w.astype(BF16), xv_w.astype(BF16))
    mem_k = cache_mem_k.reshape(L, Bd, n_mem, E)
    mem_v = cache_mem_v.reshape(L, Bd, n_mem, E)

    x = jnp.concatenate([x_prompt.reshape(Mp, D), x_sample.reshape(Ms, D)], axis=0)
    conv_p, conv_s, kvl_all, kr_all = [], [], [], []
    for l in range(L):
        u, ckv, kr, ckvb, krb, qlat, qrope = _in_proj(x, w, l, cos, sin, scale)
        a_p, tail_p = _conv_prompt(u, w, l, Bp, S)
        a_s, tail_s = _conv_sample(state_conv, u[Mp:].reshape(Bd, T, conv_ch), w, l)
        o_p = _attn_prompt(qlat, qrope, ckvb, krb, Bp, S)
        o_s = _attn_sample(page_table, qlat[:, Mp:].astype(F32), qrope[:, Mp:].astype(F32), ckv[Mp:], kr[Mp:],
                           cache_kv_latent, cache_k_rope, l, T)
        a = jnp.concatenate([a_p, a_s.reshape(Ms, conv_ch).astype(BF16)], axis=0)
        olat = jnp.concatenate([o_p, o_s.astype(BF16)], axis=1)
        x = _out_proj(x, a, olat, w, l)
        x = _xattn_prompt(x, mk_all, mv_all, w, l, Bp, S, x_heads)
        x = _xattn_sample(x, mem_k, mem_v, w, l, Mp, T, x_heads)
        x = _ffn(x, w, l)
        conv_p.append(tail_p)
        conv_s.append(tail_s)
        kvl_all.append(ckv)
        kr_all.append(kr)

    g_f = ln_f_g.reshape(1, D)
    y_prompt = _final_norm(x, g_f, 0, Mp).reshape(Bp, S, D)
    y_sample = _final_norm(x, g_f, Mp, Ms).reshape(Bd, T, D)
    kvl_all = jnp.stack(kvl_all)
    kr_all = jnp.stack(kr_all)
    return (y_prompt, y_sample,
            jnp.stack(conv_p),
            kvl_all[:, :Mp].reshape(L, Bp, S, kvl),
            kr_all[:, :Mp].reshape(L, Bp, S, rope),
            mk_all.reshape(L, Bp, n_mem, x_heads, x_hd),
            mv_all.reshape(L, Bp, n_mem, x_heads, x_hd),
            jnp.stack(conv_s),
            kvl_all[:, Mp:].reshape(L, Bd, T, kvl),
            kr_all[:, Mp:].reshape(L, Bd, T, rope))
```

```python
import functools
import math

import jax
import jax.numpy as jnp
from jax import lax
from jax.experimental import pallas as pl
from jax.experimental.pallas import tpu as pltpu

F32 = jnp.float32
BF16 = jnp.bfloat16
EPS = 1e-6
ROPE_THETA = 10000.0
VMEM_LIMIT = 56 * 1024 * 1024
NEG = -1e30


def _dot(a, b):
    return jnp.dot(a, b, preferred_element_type=F32)


def _dot_nt(a, b):
    return lax.dot_general(a, b, (((1,), (1,)), ((), ())), preferred_element_type=F32)


def _rms(x, g):
    ms = jnp.mean(x * x, axis=-1, keepdims=True)
    return (x * lax.rsqrt(ms + EPS)) * g


def _tile(n, pref, mult=8):
    t = min(n, pref)
    while t > mult and (n % t or t % mult):
        t -= 1
    assert n % t == 0 and (t % mult == 0 or t == n), (n, pref, mult)
    return t


def _params(*sem):
    return pltpu.CompilerParams(dimension_semantics=sem, vmem_limit_bytes=VMEM_LIMIT)


def _full(shape):
    zeros = (0,) * len(shape)
    return pl.BlockSpec(shape, lambda *_: zeros)


def _layer(shape, l):
    zeros = (0,) * len(shape)
    return pl.BlockSpec((None,) + tuple(shape), lambda *_: (l,) + zeros)


def _memkv_kernel(m_ref, g_ref, wk_ref, wv_ref, k_ref, v_ref):
    h = _rms(m_ref[...], g_ref[...]).astype(BF16)
    k_ref[...] = _dot(h, wk_ref[...])
    v_ref[...] = _dot(h, wv_ref[...])


def _mem_kv(mem, g, wk, wv):
    Mm, D = mem.shape
    L, _, E = wk.shape
    tm = _tile(Mm, 512)
    return pl.pallas_call(
        _memkv_kernel,
        grid=(L, Mm // tm),
        in_specs=[pl.BlockSpec((tm, D), lambda l, i: (i, 0)),
                  pl.BlockSpec((None, 1, D), lambda l, i: (l, 0, 0)),
                  pl.BlockSpec((None, D, E), lambda l, i: (l, 0, 0)),
                  pl.BlockSpec((None, D, E), lambda l, i: (l, 0, 0))],
        out_specs=[pl.BlockSpec((None, tm, E), lambda l, i: (l, i, 0))] * 2,
        out_shape=[jax.ShapeDtypeStruct((L, Mm, E), F32)] * 2,
        compiler_params=_params("parallel", "parallel"),
        name="mem_kv",
    )(mem, g, wk, wv)


ATTN_BLOCK = 256


def _inproj_kernel(x_ref, g_ref, wval_ref, wgate_ref, wcq_ref, wckv_ref, wkr_ref, wkrs_ref,
                   qg_ref, kvg_ref, wqn_ref, wqr_ref, wqrs_ref, wqrt_ref, wqrst_ref, wukt_ref, wuk_ref,
                   cos_ref, sin_ref, cost_ref, sint_ref,
                   u_ref, ckv_ref, kr_ref, ckvb_ref, krb_ref, ckvt_ref, qlat_ref, qrope_ref, qlt_ref, qrt_ref,
                   *, scale):
    n_heads, nope, _ = wukt_ref.shape
    rope = cos_ref.shape[-1]
    tk = ckvt_ref.shape[-1]
    h = _rms(x_ref[...], g_ref[...]).astype(BF16)
    u_ref[...] = _dot(h, wval_ref[...]) * jax.nn.sigmoid(_dot(h, wgate_ref[...]))
    cqn = _rms(_dot(h, wcq_ref[...]), qg_ref[...]).astype(BF16)
    ckv = _rms(_dot(h, wckv_ref[...]), kvg_ref[...])
    ckv_ref[...] = ckv
    ckvb_ref[...] = ckv.astype(BF16)
    ckv_t = ckv.T.astype(BF16)
    for j in range(ckvt_ref.shape[0]):
        ckvt_ref[j] = ckv_t[:, j * tk:(j + 1) * tk]
    cos = cos_ref[...]
    sin = sin_ref[...]
    kr = _dot(h, wkr_ref[...]) * cos + _dot(h, wkrs_ref[...]) * sin
    kr_ref[...] = kr
    krb_ref[...] = kr.astype(BF16)
    qn = _dot(cqn, wqn_ref[...])
    qr = _dot(cqn, wqr_ref[...])
    qrs = _dot(cqn, wqrs_ref[...])
    qr_t = _dot_nt(wqrt_ref[...], cqn)
    qrs_t = _dot_nt(wqrst_ref[...], cqn)
    cos_t = cost_ref[...]
    sin_t = sint_ref[...]
    for hd in range(n_heads):
        qn_h = qn[:, hd * nope:(hd + 1) * nope].astype(BF16)
        qlat_ref[hd] = (_dot(qn_h, wukt_ref[hd]) * scale).astype(BF16)
        qlt_ref[hd] = (_dot_nt(wuk_ref[hd], qn_h) * scale).astype(BF16)
        sl = slice(hd * rope, (hd + 1) * rope)
        qrope_ref[hd] = ((qr[:, sl] * cos + qrs[:, sl] * sin) * scale).astype(BF16)
        qrt_ref[hd] = ((qr_t[sl, :] * cos_t + qrs_t[sl, :] * sin_t) * scale).astype(BF16)


def _in_proj(x, w, l, cos, sin, cos_t, sin_t, scale):
    M, D = x.shape
    tm = _tile(M, 512, ATTN_BLOCK)
    n_heads, nope, kvl = w["w_ukt"].shape[1:]
    conv_ch = w["w_val"].shape[-1]
    ql = w["w_cq"].shape[-1]
    rope = cos.shape[-1]
    nc = tm // ATTN_BLOCK
    row = lambda c: pl.BlockSpec((tm, c), lambda i: (i, 0))
    col = lambda c: pl.BlockSpec((c, tm), lambda i: (0, i))
    heads = lambda c: pl.BlockSpec((n_heads, tm, c), lambda i: (0, i, 0))
    heads_t = lambda c: pl.BlockSpec((n_heads, c, tm), lambda i: (0, 0, i))
    return pl.pallas_call(
        functools.partial(_inproj_kernel, scale=scale),
        grid=(M // tm,),
        in_specs=[row(D), _layer((1, D), l),
                  _layer((D, conv_ch), l), _layer((D, conv_ch), l), _layer((D, ql), l),
                  _layer((D, kvl), l), _layer((D, rope), l), _layer((D, rope), l),
                  _layer((1, ql), l), _layer((1, kvl), l),
                  _layer((ql, n_heads * nope), l), _layer((ql, n_heads * rope), l),
                  _layer((ql, n_heads * rope), l), _layer((n_heads * rope, ql), l),
                  _layer((n_heads * rope, ql), l), _layer((n_heads, nope, kvl), l),
                  _layer((n_heads, kvl, nope), l),
                  row(rope), row(rope), col(rope), col(rope)],
        out_specs=[row(conv_ch), row(kvl), row(rope), row(kvl), row(rope),
                   pl.BlockSpec((nc, kvl, ATTN_BLOCK), lambda i: (i, 0, 0)),
                   heads(kvl), heads(rope), heads_t(kvl), heads_t(rope)],
        out_shape=[jax.ShapeDtypeStruct((M, conv_ch), F32),
                   jax.ShapeDtypeStruct((M, kvl), F32),
                   jax.ShapeDtypeStruct((M, rope), F32),
                   jax.ShapeDtypeStruct((M, kvl), BF16),
                   jax.ShapeDtypeStruct((M, rope), BF16),
                   jax.ShapeDtypeStruct((M // ATTN_BLOCK, kvl, ATTN_BLOCK), BF16),
                   jax.ShapeDtypeStruct((n_heads, M, kvl), BF16),
                   jax.ShapeDtypeStruct((n_heads, M, rope), BF16),
                   jax.ShapeDtypeStruct((n_heads, kvl, M), BF16),
                   jax.ShapeDtypeStruct((n_heads, rope, M), BF16)],
        compiler_params=_params("parallel"),
        name="in_proj",
    )(x, w["ln_mix_g"], w["w_val"], w["w_gate"], w["w_cq"], w["w_ckv"], w["w_kr"], w["w_kr_sw"],
      w["q_norm_g"], w["kv_norm_g"], w["w_qn"], w["w_qr"], w["w_qr_sw"], w["w_qr_t"], w["w_qr_sw_t"],
      w["w_ukt"], w["w_uk_h"], cos, sin, cos_t, sin_t)


CONV_ROWS = 64
LANES = 128
SUBLANES = 8


def _ln_swish(y, g, b):
    mu = jnp.mean(y, axis=-1, keepdims=True)
    d = y - mu
    var = jnp.mean(d * d, axis=-1, keepdims=True)
    z = d * lax.rsqrt(var + EPS) * g + b
    return z * jax.nn.sigmoid(z)


def _conv_prompt_kernel(u_ref, prev_ref, w_ref, b_ref, g_ref, lb_ref, a_ref, tail_ref, hist_ref, shift_ref, y_ref,
                        *, halo):
    t = pl.program_id(1)
    tt, ch = u_ref.shape
    K = w_ref.shape[0]
    off = halo - (K - 1)

    @pl.when(t == 0)
    def _():
        hist_ref[0:halo, :] = jnp.zeros((halo, ch), F32)

    @pl.when(t > 0)
    def _():
        hist_ref[0:halo, :] = prev_ref[...]

    hist_ref[halo:halo + tt, :] = u_ref[...]
    n_sh = halo + tt - SUBLANES
    for r in range(1, SUBLANES):
        for c0 in range(0, ch, LANES):
            shift_ref[r - 1, 0:n_sh, c0:c0 + LANES] = hist_ref[r:r + n_sh, c0:c0 + LANES]
    for r0 in range(0, tt, CONV_ROWS):
        for c0 in range(0, ch, LANES):
            cs = slice(c0, c0 + LANES)
            acc = jnp.zeros((CONV_ROWS, LANES), F32)
            for k in range(K):
                r = (off + k) % SUBLANES
                base = off + k - r + r0
                rows = slice(base, base + CONV_ROWS)
                tap = hist_ref[rows, cs] if r == 0 else shift_ref[r - 1, rows, cs]
                acc = acc + w_ref[k:k + 1, cs] * tap
            y_ref[r0:r0 + CONV_ROWS, cs] = acc
    y = y_ref[...] + b_ref[...]
    a_ref[...] = _ln_swish(y, g_ref[...], lb_ref[...]).astype(BF16)
    tail_ref[...] = hist_ref[halo + tt - (K - 1):halo + tt, :]


def _conv_prompt(u, w, l, batch, seq):
    ch = u.shape[-1]
    K = w["conv_w"].shape[1]
    halo = 32
    assert K - 1 <= halo
    tt = _tile(seq, 128, CONV_ROWS)
    nt = seq // tt
    assert tt % halo == 0 and tt % CONV_ROWS == 0 and ch % LANES == 0
    hb = tt // halo
    return pl.pallas_call(
        functools.partial(_conv_prompt_kernel, halo=halo),
        grid=(batch, nt),
        in_specs=[pl.BlockSpec((tt, ch), lambda b, t: (b * nt + t, 0)),
                  pl.BlockSpec((halo, ch), lambda b, t: (jnp.maximum((b * nt + t) * hb - 1, 0), 0)),
                  _layer((K, ch), l), _layer((1, ch), l), _layer((1, ch), l), _layer((1, ch), l)],
        out_specs=[pl.BlockSpec((tt, ch), lambda b, t: (b * nt + t, 0)),
                   pl.BlockSpec((None, K - 1, ch), lambda b, t: (b, 0, 0))],
        out_shape=[jax.ShapeDtypeStruct((batch * seq, ch), BF16),
                   jax.ShapeDtypeStruct((batch, K - 1, ch), F32)],
        scratch_shapes=[pltpu.VMEM((halo + tt, ch), F32), pltpu.VMEM((SUBLANES - 1, halo + tt, ch), F32),
                        pltpu.VMEM((tt, ch), F32)],
        compiler_params=_params("parallel", "arbitrary"),
        name="conv_prompt",
    )(u, u, w["conv_w"], w["conv_b"], w["conv_ln_g"], w["conv_ln_b"])


def _conv_sample_kernel(st_ref, u_ref, w_ref, b_ref, g_ref, lb_ref, a_ref, ns_ref, hist_ref):
    bb, T, ch = u_ref.shape
    K = w_ref.shape[0]
    hist_ref[:, 0:K - 1, :] = st_ref[...]
    hist_ref[:, K - 1:K - 1 + T, :] = u_ref[...]
    for c0 in range(0, ch, LANES):
        cs = slice(c0, c0 + LANES)
        acc = jnp.zeros((bb, T, LANES), F32)
        for k in range(K):
            acc = acc + w_ref[k:k + 1, cs][None] * hist_ref[:, k:k + T, cs]
        y = acc + b_ref[:, cs][None]
        a_ref[:, :, cs] = y
    y = a_ref[...]
    a_ref[...] = _ln_swish(y, g_ref[...][None], lb_ref[...][None])
    ns_ref[...] = hist_ref[:, T:T + K - 1, :]


def _conv_sample(state, u_s, w, l):
    Bd, T, ch = u_s.shape
    K = w["conv_w"].shape[1]
    bb = _tile(Bd, 8, 1)
    hist_rows = -(-(K - 1 + T) // 8) * 8
    return pl.pallas_call(
        _conv_sample_kernel,
        grid=(Bd // bb,),
        in_specs=[pl.BlockSpec((None, bb, K - 1, ch), lambda i: (l, i, 0, 0)),
                  pl.BlockSpec((bb, T, ch), lambda i: (i, 0, 0)),
                  _layer((K, ch), l), _layer((1, ch), l), _layer((1, ch), l), _layer((1, ch), l)],
        out_specs=[pl.BlockSpec((bb, T, ch), lambda i: (i, 0, 0)),
                   pl.BlockSpec((bb, K - 1, ch), lambda i: (i, 0, 0))],
        out_shape=[jax.ShapeDtypeStruct((Bd, T, ch), F32),
                   jax.ShapeDtypeStruct((Bd, K - 1, ch), F32)],
        scratch_shapes=[pltpu.VMEM((bb, hist_rows, ch), F32)],
        compiler_params=_params("parallel"),
        name="conv_sample",
    )(state, u_s, w["conv_w"], w["conv_b"], w["conv_ln_g"], w["conv_ln_b"])


def _attn_prompt_kernel(qlt_ref, qrt_ref, k_ref, r_ref, kt_ref, o_ref, m_ref, l_ref, acc_ref):
    qi = pl.program_id(1)
    n_heads, kvl, tq = qlt_ref.shape
    tk = kt_ref.shape[-1]
    m_ref[...] = jnp.full(m_ref.shape, NEG, F32)
    l_ref[...] = jnp.zeros(l_ref.shape, F32)
    acc_ref[...] = jnp.zeros(acc_ref.shape, F32)

    def chunk(c, masked):
        ks = pl.ds(pl.multiple_of(c * tk, tk), tk)
        kc = k_ref[ks, :]
        rc = r_ref[ks, :]
        ktc = kt_ref[c]
        if masked:
            key = c * tk + lax.broadcasted_iota(jnp.int32, (tk, tq), 0)
            visible = key <= qi * tq + lax.broadcasted_iota(jnp.int32, (tk, tq), 1)
        scores = lambda hd: _dot(kc, qlt_ref[hd]) + _dot(rc, qrt_ref[hd])
        def accumulate(hd, alpha, p):
            acc_ref[hd] = alpha * acc_ref[hd] + _dot(ktc, p)

        queue = [scores(0), scores(1)]
        pending = None
        for hd in range(n_heads):
            st = queue.pop(0)
            if hd + 2 < n_heads:
                queue.append(scores(hd + 2))
            if masked:
                st = jnp.where(visible, st, NEG)
            m_old = m_ref[hd]
            m_new = jnp.maximum(m_old, jnp.max(st, axis=0, keepdims=True))
            alpha = jnp.exp(m_old - m_new)
            p = jnp.exp(st - m_new)
            l_ref[hd] = alpha * l_ref[hd] + jnp.sum(p, axis=0, keepdims=True)
            m_ref[hd] = m_new
            if pending is not None:
                accumulate(*pending)
            pending = (hd, alpha, p.astype(BF16))
        accumulate(*pending)

    def body(c, carry):
        chunk(c, False)
        return carry

    lax.fori_loop(0, qi, body, 0)
    chunk(qi, True)
    for hd in range(n_heads):
        o_ref[hd] = (acc_ref[hd] / l_ref[hd]).T.astype(BF16)


def _attn_prompt(qlat_t, qrope_t, ckvb, krb, ckv_t, batch, seq):
    n_heads, kvl, _ = qlat_t.shape
    rope = qrope_t.shape[1]
    tk = ckv_t.shape[-1]
    tq = tk
    assert seq % tq == 0
    nq = seq // tq
    return pl.pallas_call(
        _attn_prompt_kernel,
        grid=(batch, nq),
        in_specs=[pl.BlockSpec((n_heads, kvl, tq), lambda b, q: (0, 0, b * nq + q)),
                  pl.BlockSpec((n_heads, rope, tq), lambda b, q: (0, 0, b * nq + q)),
                  pl.BlockSpec((seq, kvl), lambda b, q: (b, 0)),
                  pl.BlockSpec((seq, rope), lambda b, q: (b, 0)),
                  pl.BlockSpec((nq, kvl, tk), lambda b, q: (b, 0, 0))],
        out_specs=pl.BlockSpec((n_heads, tq, kvl), lambda b, q: (0, b * nq + q, 0)),
        out_shape=jax.ShapeDtypeStruct((n_heads, batch * seq, kvl), BF16),
        scratch_shapes=[pltpu.VMEM((n_heads, 1, tq), F32), pltpu.VMEM((n_heads, 1, tq), F32),
                        pltpu.VMEM((n_heads, kvl, tq), F32)],
        compiler_params=_params("parallel", "arbitrary"),
        name="attn_prompt",
    )(qlat_t, qrope_t, ckvb, krb, ckv_t)


def _attn_sample_kernel(pt_ref, ql_ref, qr_ref, kn_ref, rn_ref, ckv_hbm, krt_hbm, o_ref,
                        kbuf, rbuf, sem, *, layer, chunk_pages):
    b = pl.program_id(0)
    nb = pl.num_programs(0)
    n_pages, page, kvl = kbuf.shape[1:]
    rope = rbuf.shape[2]
    n_heads, T, _ = ql_ref.shape
    rows = n_heads * T
    slot = lax.rem(b, 2)

    def page_copies(bb, sl, p):
        pg = pt_ref[bb, p]
        return (pltpu.make_async_copy(ckv_hbm.at[layer, pg], kbuf.at[sl, p], sem.at[0, sl]),
                pltpu.make_async_copy(krt_hbm.at[layer, pg], rbuf.at[sl, p], sem.at[1, sl]))

    def start_fetch(bb, sl):
        for p in range(n_pages):
            for cp in page_copies(bb, sl, p):
                cp.start()

    @pl.when(b == 0)
    def _():
        start_fetch(b, slot)

    @pl.when(b + 1 < nb)
    def _():
        start_fetch(b + 1, 1 - slot)

    ql = ql_ref[...].reshape(rows, kvl).astype(BF16)
    qr = qr_ref[...].reshape(rows, rope).astype(BF16)
    kn = kn_ref[...].astype(BF16)
    s_new = _dot_nt(ql, kn) + _dot_nt(qr, rn_ref[...].astype(BF16))
    q_tok = lax.rem(lax.broadcasted_iota(jnp.int32, (rows, T), 0), T)
    s_new = jnp.where(lax.broadcasted_iota(jnp.int32, (rows, T), 1) <= q_tok, s_new, NEG)
    m = jnp.max(s_new, axis=-1, keepdims=True)
    p_new = jnp.exp(s_new - m)
    den = jnp.sum(p_new, axis=-1, keepdims=True)
    acc = _dot(p_new.astype(BF16), kn)

    for p in range(n_pages):
        for cp in page_copies(b, slot, p):
            cp.wait()

    ck = chunk_pages * page
    n_chunks = n_pages // chunk_pages

    def scores(c):
        pages = range(c * chunk_pages, (c + 1) * chunk_pages)
        kc = kbuf[slot, pages.start:pages.stop].reshape(ck, kvl).astype(BF16)
        s = _dot_nt(ql, kc) + jnp.concatenate(
            [_dot(qr, rbuf[slot, p].astype(BF16)) for p in pages], axis=-1)
        return s, kc

    queue = [scores(c) for c in range(min(2, n_chunks))]
    pending = None
    for c in range(n_chunks):
        s, kc = queue.pop(0)
        if c + 2 < n_chunks:
            queue.append(scores(c + 2))
        m_new = jnp.maximum(m, jnp.max(s, axis=-1, keepdims=True))
        alpha = jnp.exp(m - m_new)
        pr = jnp.exp(s - m_new)
        den = alpha * den + jnp.sum(pr, axis=-1, keepdims=True)
        m = m_new
        if pending is not None:
            acc = pending[0] * acc + _dot(pending[1], pending[2])
        pending = (alpha, pr.astype(BF16), kc)
    acc = pending[0] * acc + _dot(pending[1], pending[2])
    o_ref[...] = (acc / den).reshape(n_heads, T, kvl)


def _attn_sample(page_table, qlat, qrope, ckv, kr, cache_kv, cache_kr_t, l, T):
    Bd, n_pages = page_table.shape
    n_heads, _, kvl = qlat.shape
    rope = qrope.shape[-1]
    page = cache_kv.shape[2]
    chunk_pages = _tile(n_pages, 8, 1)
    kernel = functools.partial(_attn_sample_kernel, layer=l, chunk_pages=chunk_pages)
    return pl.pallas_call(
        kernel,
        grid_spec=pltpu.PrefetchScalarGridSpec(
            num_scalar_prefetch=1,
            grid=(Bd,),
            in_specs=[pl.BlockSpec((n_heads, T, kvl), lambda b, pt: (0, b, 0)),
                      pl.BlockSpec((n_heads, T, rope), lambda b, pt: (0, b, 0)),
                      pl.BlockSpec((T, kvl), lambda b, pt: (b, 0)),
                      pl.BlockSpec((T, rope), lambda b, pt: (b, 0)),
                      pl.BlockSpec(memory_space=pl.ANY),
                      pl.BlockSpec(memory_space=pl.ANY)],
            out_specs=pl.BlockSpec((n_heads, T, kvl), lambda b, pt: (0, b, 0)),
            scratch_shapes=[pltpu.VMEM((2, n_pages, page, kvl), F32),
                            pltpu.VMEM((2, n_pages, rope, page), F32),
                            pltpu.SemaphoreType.DMA((2, 2))]),
        out_shape=jax.ShapeDtypeStruct((n_heads, Bd * T, kvl), F32),
        compiler_params=_params("arbitrary"),
        name="attn_sample",
    )(page_table, qlat, qrope, ckv, kr, cache_kv, cache_kr_t)


def _oproj_kernel(x_ref, a_ref, ol_ref, wuv_ref, woa_ref, woo_ref, out_ref):
    n_heads = ol_ref.shape[0]
    o = jnp.concatenate([_dot(ol_ref[hd].astype(BF16), wuv_ref[hd]).astype(BF16) for hd in range(n_heads)],
                        axis=-1)
    out_ref[...] = x_ref[...] + _dot(a_ref[...].astype(BF16), woa_ref[...]) + _dot(o, woo_ref[...])


def _out_proj(x, a, olat, w, l, row0):
    M, D = x.shape
    n_heads, rows, kvl = olat.shape
    vd = w["w_uvt"].shape[-1]
    ca = a.shape[-1]
    tm = _tile(rows, 512, 16)
    assert row0 % tm == 0
    blk0 = row0 // tm
    return pl.pallas_call(
        _oproj_kernel,
        grid=(rows // tm,),
        in_specs=[pl.BlockSpec((tm, D), lambda i: (blk0 + i, 0)),
                  pl.BlockSpec((tm, ca), lambda i: (i, 0)),
                  pl.BlockSpec((n_heads, tm, kvl), lambda i: (0, i, 0)),
                  _layer((n_heads, kvl, vd), l), _layer((ca, D), l), _layer((n_heads * vd, D), l)],
        out_specs=pl.BlockSpec((tm, D), lambda i: (blk0 + i, 0)),
        out_shape=jax.ShapeDtypeStruct((M, D), F32),
        input_output_aliases={0: 0},
        compiler_params=_params("parallel"),
        name="out_proj",
    )(x, a, olat, w["w_uvt"], w["w_o_a"], w["w_o_o"])


def _softmax_pv(s, v):
    m = jnp.max(s, axis=-1, keepdims=True)
    p = jnp.exp(s - m)
    return p, jnp.sum(p, axis=-1, keepdims=True)


def _xattn_prompt_kernel(x_ref, g_ref, wq_ref, mk_ref, mv_ref, wo_ref, out_ref, *, n_heads, scale):
    x = x_ref[...]
    h = _rms(x, g_ref[...]).astype(BF16)
    q = (_dot(h, wq_ref[...]) * scale).astype(BF16)
    hd = q.shape[-1] // n_heads
    outs = []
    for i in range(n_heads):
        cs = slice(i * hd, (i + 1) * hd)
        s = _dot_nt(q[:, cs], mk_ref[:, cs].astype(BF16))
        m = jnp.max(s, axis=-1, keepdims=True)
        p = jnp.exp(s - m)
        den = jnp.sum(p, axis=-1, keepdims=True)
        outs.append((_dot(p.astype(BF16), mv_ref[:, cs].astype(BF16)) / den).astype(BF16))
    out_ref[...] = x + _dot(jnp.concatenate(outs, axis=-1), wo_ref[...])


def _xattn_prompt(x, mk, mv, w, l, batch, seq, n_heads):
    M, D = x.shape
    E = mk.shape[-1]
    n_mem = mk.shape[1] // batch
    tm = _tile(seq, 512, 16)
    per = seq // tm
    kernel = functools.partial(_xattn_prompt_kernel, n_heads=n_heads, scale=(E // n_heads) ** -0.5)
    return pl.pallas_call(
        kernel,
        grid=(batch * per,),
        in_specs=[pl.BlockSpec((tm, D), lambda i: (i, 0)), _layer((1, D), l), _layer((D, E), l),
                  pl.BlockSpec((None, n_mem, E), lambda i: (l, i // per, 0)),
                  pl.BlockSpec((None, n_mem, E), lambda i: (l, i // per, 0)),
                  _layer((E, D), l)],
        out_specs=pl.BlockSpec((tm, D), lambda i: (i, 0)),
        out_shape=jax.ShapeDtypeStruct((M, D), F32),
        input_output_aliases={0: 0},
        compiler_params=_params("parallel"),
        name="xattn_prompt",
    )(x, w["ln_x_g"], w["xq_w"], mk, mv, w["xo_w"])


def _xattn_sample_kernel(x_ref, g_ref, wq_ref, mk_ref, mv_ref, wo_ref, out_ref, *, n_heads, scale):
    x = x_ref[...]
    bb, kv_rows, hd = mk_ref.shape
    T = x.shape[0] // bb
    h = _rms(x, g_ref[...]).astype(BF16)
    q = (_dot(h, wq_ref[...]) * scale).reshape(bb, T, n_heads * hd)
    qs = jnp.concatenate([q[:, :, i * hd:(i + 1) * hd] for i in range(n_heads)], axis=1)
    s = jnp.einsum("bqd,bkd->bqk", qs.astype(BF16), mk_ref[...].astype(BF16), preferred_element_type=F32)
    q_head = lax.broadcasted_iota(jnp.int32, s.shape, 1) // T
    k_head = lax.rem(lax.broadcasted_iota(jnp.int32, s.shape, 2), n_heads)
    s = jnp.where(q_head == k_head, s, NEG)
    m = jnp.max(s, axis=-1, keepdims=True)
    p = jnp.exp(s - m)
    den = jnp.sum(p, axis=-1, keepdims=True)
    o = jnp.einsum("bqk,bkd->bqd", p.astype(BF16), mv_ref[...].astype(BF16), preferred_element_type=F32) / den
    o = jnp.concatenate([o[:, i * T:(i + 1) * T, :] for i in range(n_heads)], axis=-1)
    out_ref[...] = x + _dot(o.reshape(bb * T, n_heads * hd).astype(BF16), wo_ref[...])


def _xattn_sample(x, mem_k, mem_v, w, l, row0, T, n_heads):
    M, D = x.shape
    _, Bd, kv_rows, hd = mem_k.shape
    E = n_heads * hd
    bb = _tile(Bd, 8, 1)
    tm = bb * T
    assert row0 % tm == 0 and tm % 8 == 0
    blk0 = row0 // tm
    kernel = functools.partial(_xattn_sample_kernel, n_heads=n_heads, scale=hd ** -0.5)
    return pl.pallas_call(
        kernel,
        grid=(Bd // bb,),
        in_specs=[pl.BlockSpec((tm, D), lambda i: (blk0 + i, 0)), _layer((1, D), l), _layer((D, E), l),
                  pl.BlockSpec((None, bb, kv_rows, hd), lambda i: (l, i, 0, 0)),
                  pl.BlockSpec((None, bb, kv_rows, hd), lambda i: (l, i, 0, 0)),
                  _layer((E, D), l)],
        out_specs=pl.BlockSpec((tm, D), lambda i: (blk0 + i, 0)),
        out_shape=jax.ShapeDtypeStruct((M, D), F32),
        input_output_aliases={0: 0},
        compiler_params=_params("parallel"),
        name="xattn_sample",
    )(x, w["ln_x_g"], w["xq_w"], mem_k, mem_v, w["xo_w"])


def _ffn_kernel(x_ref, g_ref, wg_ref, wu_ref, wd_ref, out_ref, h_ref, acc_ref):
    f = pl.program_id(1)

    @pl.when(f == 0)
    def _():
        h_ref[...] = _rms(x_ref[...], g_ref[...]).astype(BF16)
        acc_ref[...] = jnp.zeros_like(acc_ref)

    h = h_ref[...]
    gate = _dot(h, wg_ref[...])
    act = (gate * jax.nn.sigmoid(gate) * _dot(h, wu_ref[...])).astype(BF16)
    acc_ref[...] += _dot(act, wd_ref[...])

    @pl.when(f == pl.num_programs(1) - 1)
    def _():
        out_ref[...] = x_ref[...] + acc_ref[...]


def _ffn(x, w, l):
    M, D = x.shape
    Fd = w["ffn_w_gate"].shape[-1]
    tm = _tile(M, 512, 16)
    tf = _tile(Fd, 512, 128)
    return pl.pallas_call(
        _ffn_kernel,
        grid=(M // tm, Fd // tf),
        in_specs=[pl.BlockSpec((tm, D), lambda i, f: (i, 0)), _layer((1, D), l),
                  pl.BlockSpec((None, D, tf), lambda i, f: (l, 0, f)),
                  pl.BlockSpec((None, D, tf), lambda i, f: (l, 0, f)),
                  pl.BlockSpec((None, tf, D), lambda i, f: (l, f, 0))],
        out_specs=pl.BlockSpec((tm, D), lambda i, f: (i, 0)),
        out_shape=jax.ShapeDtypeStruct((M, D), F32),
        scratch_shapes=[pltpu.VMEM((tm, D), BF16), pltpu.VMEM((tm, D), F32)],
        input_output_aliases={0: 0},
        compiler_params=_params("parallel", "arbitrary"),
        name="ffn",
    )(x, w["ln_ffn_g"], w["ffn_w_gate"], w["ffn_w_up"], w["ffn_w_down"])


def _norm_kernel(x_ref, g_ref, o_ref):
    o_ref[...] = _rms(x_ref[...], g_ref[...])


def _final_norm(x, g, row0, rows):
    M, D = x.shape
    tm = _tile(rows, 512)
    assert row0 % tm == 0
    blk0 = row0 // tm
    return pl.pallas_call(
        _norm_kernel,
        grid=(rows // tm,),
        in_specs=[pl.BlockSpec((tm, D), lambda i: (blk0 + i, 0)), _full((1, D))],
        out_specs=pl.BlockSpec((tm, D), lambda i: (i, 0)),
        out_shape=jax.ShapeDtypeStruct((rows, D), F32),
        compiler_params=_params("parallel"),
        name="final_norm",
    )(x, g)


def _half_swap_cols(w, width):
    shp = w.shape
    g = w.reshape(shp[:-1] + (shp[-1] // width, 2, width // 2))
    return jnp.flip(g, axis=-2).reshape(shp)


def _rope_tables(pos, rope):
    half = rope // 2
    inv = jnp.power(ROPE_THETA, -jnp.arange(half, dtype=F32) / half)
    ang = pos.astype(F32)[:, None] * inv[None, :]
    cos, sin = jnp.cos(ang), jnp.sin(ang)
    return jnp.concatenate([cos, cos], axis=-1), jnp.concatenate([-sin, sin], axis=-1)


def kernel(x_prompt, x_sample, mem_prompt, cache_kv_latent, cache_k_rope, cache_mem_k, cache_mem_v, state_conv, page_table, ln_mix_g, w_in, q_norm_g, w_uq, kv_norm_g, w_uk, w_uv, conv_w, conv_b, conv_ln_g, conv_ln_b, w_o, ln_x_g, ln_mem_g, xq_w, xk_w, xv_w, xo_w, ln_ffn_g, ffn_w_gate, ffn_w_up, ffn_w_down, ln_f_g):
    Bp, S, D = x_prompt.shape
    Bd, T, _ = x_sample.shape
    L = w_in.shape[0]
    n_mem = mem_prompt.shape[1]
    page = cache_kv_latent.shape[2]
    kvl = cache_kv_latent.shape[3]
    rope = cache_k_rope.shape[3]
    n_heads, nope = w_uk.shape[2], w_uk.shape[3]
    vd = w_uv.shape[3]
    conv_ch = conv_w.shape[2]
    K = conv_w.shape[1]
    ql = q_norm_g.shape[1]
    x_heads, x_hd = cache_mem_k.shape[3], cache_mem_k.shape[4]
    E = x_heads * x_hd
    past_len = page_table.shape[1] * page
    Mp, Ms = Bp * S, Bd * T
    scale = float(nope + rope) ** -0.5

    vec = lambda g: g.reshape(L, 1, -1)
    w_uq_h = w_uq.reshape(L, ql, n_heads, nope + rope)
    w_qr = w_uq_h[..., nope:].reshape(L, ql, n_heads * rope)
    w_kr = w_in[:, :, 2 * conv_ch + ql + kvl:]
    w = {
        "ln_mix_g": vec(ln_mix_g), "q_norm_g": vec(q_norm_g), "kv_norm_g": vec(kv_norm_g),
        "w_val": w_in[:, :, :conv_ch].astype(BF16),
        "w_gate": w_in[:, :, conv_ch:2 * conv_ch].astype(BF16),
        "w_cq": w_in[:, :, 2 * conv_ch:2 * conv_ch + ql].astype(BF16),
        "w_ckv": w_in[:, :, 2 * conv_ch + ql:2 * conv_ch + ql + kvl].astype(BF16),
        "w_kr": w_kr.astype(BF16),
        "w_kr_sw": _half_swap_cols(w_kr, rope).astype(BF16),
        "w_qn": w_uq_h[..., :nope].reshape(L, ql, n_heads * nope).astype(BF16),
        "w_qr": w_qr.astype(BF16),
        "w_qr_sw": _half_swap_cols(w_qr, rope).astype(BF16),
        "w_qr_t": jnp.swapaxes(w_qr, 1, 2).astype(BF16),
        "w_qr_sw_t": jnp.swapaxes(_half_swap_cols(w_qr, rope), 1, 2).astype(BF16),
        "w_ukt": jnp.transpose(w_uk, (0, 2, 3, 1)).astype(BF16),
        "w_uk_h": jnp.transpose(w_uk, (0, 2, 1, 3)).astype(BF16),
        "w_uvt": jnp.transpose(w_uv, (0, 2, 1, 3)).astype(BF16),
        "conv_w": conv_w, "conv_b": vec(conv_b), "conv_ln_g": vec(conv_ln_g), "conv_ln_b": vec(conv_ln_b),
        "w_o_a": w_o[:, :conv_ch].astype(BF16), "w_o_o": w_o[:, conv_ch:].astype(BF16),
        "ln_x_g": vec(ln_x_g), "xq_w": xq_w.astype(BF16), "xo_w": xo_w.astype(BF16),
        "ln_ffn_g": vec(ln_ffn_g), "ffn_w_gate": ffn_w_gate.astype(BF16),
        "ffn_w_up": ffn_w_up.astype(BF16), "ffn_w_down": ffn_w_down.astype(BF16),
    }
    pos = jnp.concatenate([jnp.tile(jnp.arange(S, dtype=jnp.int32), Bp),
                           jnp.tile(past_len + jnp.arange(T, dtype=jnp.int32), Bd)])
    cos, sin = _rope_tables(pos, rope)
    cos_t, sin_t = cos.T, sin.T

    mk_all, mv_all = _mem_kv(mem_prompt.reshape(Bp * n_mem, D), vec(ln_mem_g),
                             xk_w.astype(BF16), xv_w.astype(BF16))
    mem_k = cache_mem_k.reshape(L, Bd, n_mem * x_heads, x_hd)
    mem_v = cache_mem_v.reshape(L, Bd, n_mem * x_heads, x_hd)
    cache_kr_t = jnp.swapaxes(cache_k_rope, 2, 3)

    x = jnp.concatenate([x_prompt.reshape(Mp, D), x_sample.reshape(Ms, D)], axis=0)
    conv_p, conv_s, kvl_all, kr_all = [], [], [], []
    for l in range(L):
        u, ckv, kr, ckvb, krb, ckv_t, qlat, qrope, qlat_t, qrope_t = _in_proj(x, w, l, cos, sin, cos_t, sin_t, scale)
        a_p, tail_p = _conv_prompt(u, w, l, Bp, S)
        a_s, tail_s = _conv_sample(state_conv, u[Mp:].reshape(Bd, T, conv_ch), w, l)
        o_p = _attn_prompt(qlat_t, qrope_t, ckvb, krb, ckv_t, Bp, S)
        o_s = _attn_sample(page_table, qlat[:, Mp:].astype(F32), qrope[:, Mp:].astype(F32), ckv[Mp:], kr[Mp:],
                           cache_kv_latent, cache_kr_t, l, T)
        x = _out_proj(x, a_p, o_p, w, l, 0)
        x = _out_proj(x, a_s.reshape(Ms, conv_ch), o_s, w, l, Mp)
        x = _xattn_prompt(x, mk_all, mv_all, w, l, Bp, S, x_heads)
        x = _xattn_sample(x, mem_k, mem_v, w, l, Mp, T, x_heads)
        x = _ffn(x, w, l)
        conv_p.append(tail_p)
        conv_s.append(tail_s)
        kvl_all.append(ckv)
        kr_all.append(kr)

    g_f = ln_f_g.reshape(1, D)
    y_prompt = _final_norm(x, g_f, 0, Mp).reshape(Bp, S, D)
    y_sample = _final_norm(x, g_f, Mp, Ms).reshape(Bd, T, D)
    kvl_all = jnp.stack(kvl_all)
    kr_all = jnp.stack(kr_all)
    return (y_prompt, y_sample,
            jnp.stack(conv_p),
            kvl_all[:, :Mp].reshape(L, Bp, S, kvl),
            kr_all[:, :Mp].reshape(L, Bp, S, rope),
            mk_all.reshape(L, Bp, n_mem, x_heads, x_hd),
            mv_all.reshape(L, Bp, n_mem, x_heads, x_hd),
            jnp.stack(conv_s),
            kvl_all[:, Mp:].reshape(L, Bd, T, kvl),
            kr_all[:, Mp:].reshape(L, Bd, T, rope))
```

```python
import functools
import math

import jax
import jax.numpy as jnp
from jax import lax
from jax.experimental import pallas as pl
from jax.experimental.pallas import tpu as pltpu

F32 = jnp.float32
BF16 = jnp.bfloat16
EPS = 1e-6
ROPE_THETA = 10000.0
VMEM_LIMIT = 56 * 1024 * 1024
NEG = -1e30


def _dot(a, b):
    return jnp.dot(a, b, preferred_element_type=F32)


def _dot_nt(a, b):
    return lax.dot_general(a, b, (((1,), (1,)), ((), ())), preferred_element_type=F32)


def _rms(x, g):
    ms = jnp.mean(x * x, axis=-1, keepdims=True)
    return (x * lax.rsqrt(ms + EPS)) * g


def _tile(n, pref, mult=8):
    t = min(n, pref)
    while t > mult and (n % t or t % mult):
        t -= 1
    assert n % t == 0 and (t % mult == 0 or t == n), (n, pref, mult)
    return t


def _params(*sem):
    return pltpu.CompilerParams(dimension_semantics=sem, vmem_limit_bytes=VMEM_LIMIT)


def _full(shape):
    zeros = (0,) * len(shape)
    return pl.BlockSpec(shape, lambda *_: zeros)


def _layer(shape, l):
    zeros = (0,) * len(shape)
    return pl.BlockSpec((None,) + tuple(shape), lambda *_: (l,) + zeros)


def _memkv_kernel(m_ref, g_ref, wk_ref, wv_ref, k_ref, v_ref):
    h = _rms(m_ref[...], g_ref[...]).astype(BF16)
    k_ref[...] = _dot(h, wk_ref[...])
    v_ref[...] = _dot(h, wv_ref[...])


def _mem_kv(mem, g, wk, wv):
    Mm, D = mem.shape
    L, _, E = wk.shape
    tm = _tile(Mm, 512)
    return pl.pallas_call(
        _memkv_kernel,
        grid=(L, Mm // tm),
        in_specs=[pl.BlockSpec((tm, D), lambda l, i: (i, 0)),
                  pl.BlockSpec((None, 1, D), lambda l, i: (l, 0, 0)),
                  pl.BlockSpec((None, D, E), lambda l, i: (l, 0, 0)),
                  pl.BlockSpec((None, D, E), lambda l, i: (l, 0, 0))],
        out_specs=[pl.BlockSpec((None, tm, E), lambda l, i: (l, i, 0))] * 2,
        out_shape=[jax.ShapeDtypeStruct((L, Mm, E), F32)] * 2,
        compiler_params=_params("parallel", "parallel"),
        name="mem_kv",
    )(mem, g, wk, wv)


ATTN_BLOCK = 256


def _inproj_kernel(x_ref, g_ref, wval_ref, wgate_ref, wcq_ref, wckv_ref, wkr_ref, wkrs_ref,
                   qg_ref, kvg_ref, wqn_ref, wqr_ref, wqrs_ref, wqrt_ref, wqrst_ref, wukt_ref, wuk_ref,
                   cos_ref, sin_ref, cost_ref, sint_ref,
                   u_ref, ckv_ref, kr_ref, ckvb_ref, krb_ref, ckvt_ref, qlat_ref, qrope_ref, qlt_ref, qrt_ref,
                   *, scale):
    n_heads, nope, _ = wukt_ref.shape
    rope = cos_ref.shape[-1]
    tk = ckvt_ref.shape[-1]
    h = _rms(x_ref[...], g_ref[...]).astype(BF16)
    u_ref[...] = _dot(h, wval_ref[...]) * jax.nn.sigmoid(_dot(h, wgate_ref[...]))
    cqn = _rms(_dot(h, wcq_ref[...]), qg_ref[...]).astype(BF16)
    ckv = _rms(_dot(h, wckv_ref[...]), kvg_ref[...])
    ckv_ref[...] = ckv
    ckvb_ref[...] = ckv.astype(BF16)
    ckv_t = ckv.T.astype(BF16)
    for j in range(ckvt_ref.shape[0]):
        ckvt_ref[j] = ckv_t[:, j * tk:(j + 1) * tk]
    cos = cos_ref[...]
    sin = sin_ref[...]
    kr = _dot(h, wkr_ref[...]) * cos + _dot(h, wkrs_ref[...]) * sin
    kr_ref[...] = kr
    krb_ref[...] = kr.astype(BF16)
    qn = _dot(cqn, wqn_ref[...])
    qr = _dot(cqn, wqr_ref[...])
    qrs = _dot(cqn, wqrs_ref[...])
    qr_t = _dot_nt(wqrt_ref[...], cqn)
    qrs_t = _dot_nt(wqrst_ref[...], cqn)
    cos_t = cost_ref[...]
    sin_t = sint_ref[...]
    for hd in range(n_heads):
        qn_h = qn[:, hd * nope:(hd + 1) * nope].astype(BF16)
        qlat_ref[hd] = (_dot(qn_h, wukt_ref[hd]) * scale).astype(BF16)
        qlt_ref[hd] = (_dot_nt(wuk_ref[hd], qn_h) * scale).astype(BF16)
        sl = slice(hd * rope, (hd + 1) * rope)
        qrope_ref[hd] = ((qr[:, sl] * cos + qrs[:, sl] * sin) * scale).astype(BF16)
        qrt_ref[hd] = ((qr_t[sl, :] * cos_t + qrs_t[sl, :] * sin_t) * scale).astype(BF16)


def _in_proj(x, w, l, cos, sin, cos_t, sin_t, scale):
    M, D = x.shape
    tm = _tile(M, 512, ATTN_BLOCK)
    n_heads, nope, kvl = w["w_ukt"].shape[1:]
    conv_ch = w["w_val"].shape[-1]
    ql = w["w_cq"].shape[-1]
    rope = cos.shape[-1]
    nc = tm // ATTN_BLOCK
    row = lambda c: pl.BlockSpec((tm, c), lambda i: (i, 0))
    col = lambda c: pl.BlockSpec((c, tm), lambda i: (0, i))
    heads = lambda c: pl.BlockSpec((n_heads, tm, c), lambda i: (0, i, 0))
    heads_t = lambda c: pl.BlockSpec((n_heads, c, tm), lambda i: (0, 0, i))
    return pl.pallas_call(
        functools.partial(_inproj_kernel, scale=scale),
        grid=(M // tm,),
        in_specs=[row(D), _layer((1, D), l),
                  _layer((D, conv_ch), l), _layer((D, conv_ch), l), _layer((D, ql), l),
                  _layer((D, kvl), l), _layer((D, rope), l), _layer((D, rope), l),
                  _layer((1, ql), l), _layer((1, kvl), l),
                  _layer((ql, n_heads * nope), l), _layer((ql, n_heads * rope), l),
                  _layer((ql, n_heads * rope), l), _layer((n_heads * rope, ql), l),
                  _layer((n_heads * rope, ql), l), _layer((n_heads, nope, kvl), l),
                  _layer((n_heads, kvl, nope), l),
                  row(rope), row(rope), col(rope), col(rope)],
        out_specs=[row(conv_ch), row(kvl), row(rope), row(kvl), row(rope),
                   pl.BlockSpec((nc, kvl, ATTN_BLOCK), lambda i: (i, 0, 0)),
                   heads(kvl), heads(rope), heads_t(kvl), heads_t(rope)],
        out_shape=[jax.ShapeDtypeStruct((M, conv_ch), F32),
                   jax.ShapeDtypeStruct((M, kvl), F32),
                   jax.ShapeDtypeStruct((M, rope), F32),
                   jax.ShapeDtypeStruct((M, kvl), BF16),
                   jax.ShapeDtypeStruct((M, rope), BF16),
                   jax.ShapeDtypeStruct((M // ATTN_BLOCK, kvl, ATTN_BLOCK), BF16),
                   jax.ShapeDtypeStruct((n_heads, M, kvl), BF16),
                   jax.ShapeDtypeStruct((n_heads, M, rope), BF16),
                   jax.ShapeDtypeStruct((n_heads, kvl, M), BF16),
                   jax.ShapeDtypeStruct((n_heads, rope, M), BF16)],
        compiler_params=_params("parallel"),
        name="in_proj",
    )(x, w["ln_mix_g"], w["w_val"], w["w_gate"], w["w_cq"], w["w_ckv"], w["w_kr"], w["w_kr_sw"],
      w["q_norm_g"], w["kv_norm_g"], w["w_qn"], w["w_qr"], w["w_qr_sw"], w["w_qr_t"], w["w_qr_sw_t"],
      w["w_ukt"], w["w_uk_h"], cos, sin, cos_t, sin_t)


CONV_ROWS = 64
LANES = 128
SUBLANES = 8


def _ln_swish(y, g, b):
    mu = jnp.mean(y, axis=-1, keepdims=True)
    d = y - mu
    var = jnp.mean(d * d, axis=-1, keepdims=True)
    z = d * lax.rsqrt(var + EPS) * g + b
    return z * jax.nn.sigmoid(z)


def _conv_prompt_kernel(u_ref, prev_ref, w_ref, b_ref, g_ref, lb_ref, a_ref, tail_ref, hist_ref, shift_ref, y_ref,
                        *, halo):
    t = pl.program_id(1)
    tt, ch = u_ref.shape
    K = w_ref.shape[0]
    off = halo - (K - 1)

    @pl.when(t == 0)
    def _():
        hist_ref[0:halo, :] = jnp.zeros((halo, ch), F32)

    @pl.when(t > 0)
    def _():
        hist_ref[0:halo, :] = prev_ref[...]

    hist_ref[halo:halo + tt, :] = u_ref[...]
    n_sh = halo + tt - SUBLANES
    for r in range(1, SUBLANES):
        for c0 in range(0, ch, LANES):
            shift_ref[r - 1, 0:n_sh, c0:c0 + LANES] = hist_ref[r:r + n_sh, c0:c0 + LANES]
    for r0 in range(0, tt, CONV_ROWS):
        for c0 in range(0, ch, LANES):
            cs = slice(c0, c0 + LANES)
            acc = jnp.zeros((CONV_ROWS, LANES), F32)
            for k in range(K):
                r = (off + k) % SUBLANES
                base = off + k - r + r0
                rows = slice(base, base + CONV_ROWS)
                tap = hist_ref[rows, cs] if r == 0 else shift_ref[r - 1, rows, cs]
                acc = acc + w_ref[k:k + 1, cs] * tap
            y_ref[r0:r0 + CONV_ROWS, cs] = acc
    y = y_ref[...] + b_ref[...]
    a_ref[...] = _ln_swish(y, g_ref[...], lb_ref[...]).astype(BF16)
    tail_ref[...] = hist_ref[halo + tt - (K - 1):halo + tt, :]


def _conv_prompt(u, w, l, batch, seq):
    ch = u.shape[-1]
    K = w["conv_w"].shape[1]
    halo = 32
    assert K - 1 <= halo
    tt = _tile(seq, 128, CONV_ROWS)
    nt = seq // tt
    assert tt % halo == 0 and tt % CONV_ROWS == 0 and ch % LANES == 0
    hb = tt // halo
    return pl.pallas_call(
        functools.partial(_conv_prompt_kernel, halo=halo),
        grid=(batch, nt),
        in_specs=[pl.BlockSpec((tt, ch), lambda b, t: (b * nt + t, 0)),
                  pl.BlockSpec((halo, ch), lambda b, t: (jnp.maximum((b * nt + t) * hb - 1, 0), 0)),
                  _layer((K, ch), l), _layer((1, ch), l), _layer((1, ch), l), _layer((1, ch), l)],
        out_specs=[pl.BlockSpec((tt, ch), lambda b, t: (b * nt + t, 0)),
                   pl.BlockSpec((None, K - 1, ch), lambda b, t: (b, 0, 0))],
        out_shape=[jax.ShapeDtypeStruct((batch * seq, ch), BF16),
                   jax.ShapeDtypeStruct((batch, K - 1, ch), F32)],
        scratch_shapes=[pltpu.VMEM((halo + tt, ch), F32), pltpu.VMEM((SUBLANES - 1, halo + tt, ch), F32),
                        pltpu.VMEM((tt, ch), F32)],
        compiler_params=_params("parallel", "arbitrary"),
        name="conv_prompt",
    )(u, u, w["conv_w"], w["conv_b"], w["conv_ln_g"], w["conv_ln_b"])


def _conv_sample_kernel(st_ref, u_ref, w_ref, b_ref, g_ref, lb_ref, a_ref, ns_ref, hist_ref):
    bb, T, ch = u_ref.shape
    K = w_ref.shape[0]
    hist_ref[0:K - 1] = st_ref[...]
    for t in range(T):
        hist_ref[K - 1 + t] = u_ref[:, t, :]
    for t in range(T):
        cols = []
        for c0 in range(0, ch, LANES):
            cs = slice(c0, c0 + LANES)
            acc = jnp.zeros((bb, LANES), F32)
            for k in range(K):
                acc = acc + w_ref[k:k + 1, cs] * hist_ref[t + k, :, cs]
            cols.append(acc)
        y = jnp.concatenate(cols, axis=-1) + b_ref[...]
        a_ref[:, t, :] = _ln_swish(y, g_ref[...], lb_ref[...])
    ns_ref[...] = hist_ref[T:T + K - 1]


def _conv_sample(state_t, u_s, w, l):
    Bd, T, ch = u_s.shape
    K = w["conv_w"].shape[1]
    bb = _tile(Bd, SUBLANES, SUBLANES)
    return pl.pallas_call(
        _conv_sample_kernel,
        grid=(Bd // bb,),
        in_specs=[pl.BlockSpec((None, K - 1, bb, ch), lambda i: (l, 0, i, 0)),
                  pl.BlockSpec((bb, T, ch), lambda i: (i, 0, 0)),
                  _layer((K, ch), l), _layer((1, ch), l), _layer((1, ch), l), _layer((1, ch), l)],
        out_specs=[pl.BlockSpec((bb, T, ch), lambda i: (i, 0, 0)),
                   pl.BlockSpec((K - 1, bb, ch), lambda i: (0, i, 0))],
        out_shape=[jax.ShapeDtypeStruct((Bd, T, ch), F32),
                   jax.ShapeDtypeStruct((K - 1, Bd, ch), F32)],
        scratch_shapes=[pltpu.VMEM((K - 1 + T, bb, ch), F32)],
        compiler_params=_params("parallel"),
        name="conv_sample",
    )(state_t, u_s, w["conv_w"], w["conv_b"], w["conv_ln_g"], w["conv_ln_b"])


def _attn_prompt_kernel(qlt_ref, qrt_ref, k_ref, r_ref, kt_ref, o_ref, m_ref, l_ref, acc_ref):
    qi = pl.program_id(1)
    n_heads, kvl, tq = qlt_ref.shape
    tk = kt_ref.shape[-1]
    m_ref[...] = jnp.full(m_ref.shape, NEG, F32)
    l_ref[...] = jnp.zeros(l_ref.shape, F32)
    acc_ref[...] = jnp.zeros(acc_ref.shape, F32)

    def run_chunks(chunks, masked):
        loaded = {}

        def operands(ci):
            if ci not in loaded:
                c = chunks[ci]
                ks = pl.ds(pl.multiple_of(c * tk, tk), tk)
                loaded[ci] = (k_ref[ks, :], r_ref[ks, :], kt_ref[c])
            return loaded[ci]

        def scores(ci, hd):
            kc, rc, _ = operands(ci)
            return _dot(kc, qlt_ref[hd]) + _dot(rc, qrt_ref[hd])

        def accumulate(ci, hd, alpha, p):
            acc_ref[hd] = alpha * acc_ref[hd] + _dot(operands(ci)[2], p)

        if masked:
            key = chunks[0] * tk + lax.broadcasted_iota(jnp.int32, (tk, tq), 0)
            visible = key <= qi * tq + lax.broadcasted_iota(jnp.int32, (tk, tq), 1)
        units = [(ci, hd) for ci in range(len(chunks)) for hd in range(n_heads)]
        queue = [scores(*u) for u in units[:2]]
        pending = None
        for i, (c, hd) in enumerate(units):
            st = queue.pop(0)
            if i + 2 < len(units):
                queue.append(scores(*units[i + 2]))
            if masked:
                st = jnp.where(visible, st, NEG)
            m_old = m_ref[hd]
            m_new = jnp.maximum(m_old, jnp.max(st, axis=0, keepdims=True))
            alpha = jnp.exp(m_old - m_new)
            p = jnp.exp(st - m_new)
            l_ref[hd] = alpha * l_ref[hd] + jnp.sum(p, axis=0, keepdims=True)
            m_ref[hd] = m_new
            if pending is not None:
                accumulate(*pending)
            pending = (c, hd, alpha, p.astype(BF16))
        accumulate(*pending)

    def pair(j, carry):
        run_chunks([2 * j, 2 * j + 1], False)
        return carry

    lax.fori_loop(0, qi // 2, pair, 0)

    @pl.when(lax.rem(qi, 2) == 1)
    def _():
        run_chunks([qi - 1], False)

    run_chunks([qi], True)
    for hd in range(n_heads):
        o_ref[hd] = (acc_ref[hd] / l_ref[hd]).T.astype(BF16)


def _attn_prompt(qlat_t, qrope_t, ckvb, krb, ckv_t, batch, seq):
    n_heads, kvl, _ = qlat_t.shape
    rope = qrope_t.shape[1]
    tk = ckv_t.shape[-1]
    tq = tk
    assert seq % tq == 0
    nq = seq // tq
    return pl.pallas_call(
        _attn_prompt_kernel,
        grid=(batch, nq),
        in_specs=[pl.BlockSpec((n_heads, kvl, tq), lambda b, q: (0, 0, b * nq + q)),
                  pl.BlockSpec((n_heads, rope, tq), lambda b, q: (0, 0, b * nq + q)),
                  pl.BlockSpec((seq, kvl), lambda b, q: (b, 0)),
                  pl.BlockSpec((seq, rope), lambda b, q: (b, 0)),
                  pl.BlockSpec((nq, kvl, tk), lambda b, q: (b, 0, 0))],
        out_specs=pl.BlockSpec((n_heads, tq, kvl), lambda b, q: (0, b * nq + q, 0)),
        out_shape=jax.ShapeDtypeStruct((n_heads, batch * seq, kvl), BF16),
        scratch_shapes=[pltpu.VMEM((n_heads, 1, tq), F32), pltpu.VMEM((n_heads, 1, tq), F32),
                        pltpu.VMEM((n_heads, kvl, tq), F32)],
        compiler_params=_params("parallel", "arbitrary"),
        name="attn_prompt",
    )(qlat_t, qrope_t, ckvb, krb, ckv_t)


def _attn_sample_kernel(pt_ref, ql_ref, qr_ref, kn_ref, rn_ref, ckv_hbm, krt_hbm, o_ref,
                        kbuf, rbuf, sem, *, layer, chunk_pages):
    b = pl.program_id(0)
    nb = pl.num_programs(0)
    n_pages, page, kvl = kbuf.shape[1:]
    rope = rbuf.shape[2]
    n_heads, T, _ = ql_ref.shape
    rows = n_heads * T
    slot = lax.rem(b, 2)

    def page_copies(bb, sl, p):
        pg = pt_ref[bb, p]
        return (pltpu.make_async_copy(ckv_hbm.at[layer, pg], kbuf.at[sl, p], sem.at[0, sl]),
                pltpu.make_async_copy(krt_hbm.at[layer, pg], rbuf.at[sl, p], sem.at[1, sl]))

    def start_fetch(bb, sl):
        for p in range(n_pages):
            for cp in page_copies(bb, sl, p):
                cp.start()

    @pl.when(b == 0)
    def _():
        start_fetch(b, slot)

    @pl.when(b + 1 < nb)
    def _():
        start_fetch(b + 1, 1 - slot)

    ql = ql_ref[...].reshape(rows, kvl).astype(BF16)
    qr = qr_ref[...].reshape(rows, rope).astype(BF16)
    kn = kn_ref[...].astype(BF16)
    s_new = _dot_nt(ql, kn) + _dot_nt(qr, rn_ref[...].astype(BF16))
    q_tok = lax.rem(lax.broadcasted_iota(jnp.int32, (rows, T), 0), T)
    s_new = jnp.where(lax.broadcasted_iota(jnp.int32, (rows, T), 1) <= q_tok, s_new, NEG)
    m = jnp.max(s_new, axis=-1, keepdims=True)
    p_new = jnp.exp(s_new - m)
    den = jnp.sum(p_new, axis=-1, keepdims=True)
    acc = _dot(p_new.astype(BF16), kn)

    for p in range(n_pages):
        for cp in page_copies(b, slot, p):
            cp.wait()

    ck = chunk_pages * page
    n_chunks = n_pages // chunk_pages

    def scores(c):
        pages = range(c * chunk_pages, (c + 1) * chunk_pages)
        kc = kbuf[slot, pages.start:pages.stop].reshape(ck, kvl).astype(BF16)
        s = _dot_nt(ql, kc) + jnp.concatenate(
            [_dot(qr, rbuf[slot, p].astype(BF16)) for p in pages], axis=-1)
        return s, kc

    queue = [scores(c) for c in range(min(2, n_chunks))]
    pending = None
    for c in range(n_chunks):
        s, kc = queue.pop(0)
        if c + 2 < n_chunks:
            queue.append(scores(c + 2))
        m_new = jnp.maximum(m, jnp.max(s, axis=-1, keepdims=True))
        alpha = jnp.exp(m - m_new)
        pr = jnp.exp(s - m_new)
        den = alpha * den + jnp.sum(pr, axis=-1, keepdims=True)
        m = m_new
        if pending is not None:
            acc = pending[0] * acc + _dot(pending[1], pending[2])
        pending = (alpha, pr.astype(BF16), kc)
    acc = pending[0] * acc + _dot(pending[1], pending[2])
    o_ref[...] = (acc / den).reshape(n_heads, T, kvl)


def _attn_sample(page_table, qlat, qrope, ckv, kr, cache_kv, cache_kr_t, l, T):
    Bd, n_pages = page_table.shape
    n_heads, _, kvl = qlat.shape
    rope = qrope.shape[-1]
    page = cache_kv.shape[2]
    chunk_pages = _tile(n_pages, 8, 1)
    kernel = functools.partial(_attn_sample_kernel, layer=l, chunk_pages=chunk_pages)
    return pl.pallas_call(
        kernel,
        grid_spec=pltpu.PrefetchScalarGridSpec(
            num_scalar_prefetch=1,
            grid=(Bd,),
            in_specs=[pl.BlockSpec((n_heads, T, kvl), lambda b, pt: (0, b, 0)),
                      pl.BlockSpec((n_heads, T, rope), lambda b, pt: (0, b, 0)),
                      pl.BlockSpec((T, kvl), lambda b, pt: (b, 0)),
                      pl.BlockSpec((T, rope), lambda b, pt: (b, 0)),
                      pl.BlockSpec(memory_space=pl.ANY),
                      pl.BlockSpec(memory_space=pl.ANY)],
            out_specs=pl.BlockSpec((n_heads, T, kvl), lambda b, pt: (0, b, 0)),
            scratch_shapes=[pltpu.VMEM((2, n_pages, page, kvl), F32),
                            pltpu.VMEM((2, n_pages, rope, page), F32),
                            pltpu.SemaphoreType.DMA((2, 2))]),
        out_shape=jax.ShapeDtypeStruct((n_heads, Bd * T, kvl), F32),
        compiler_params=_params("arbitrary"),
        name="attn_sample",
    )(page_table, qlat, qrope, ckv, kr, cache_kv, cache_kr_t)


def _oproj_kernel(x_ref, a_ref, ol_ref, wuv_ref, woa_ref, woo_ref, out_ref):
    n_heads = ol_ref.shape[0]
    o = jnp.concatenate([_dot(ol_ref[hd].astype(BF16), wuv_ref[hd]).astype(BF16) for hd in range(n_heads)],
                        axis=-1)
    out_ref[...] = x_ref[...] + _dot(a_ref[...].astype(BF16), woa_ref[...]) + _dot(o, woo_ref[...])


def _out_proj(x, a, olat, w, l, row0):
    M, D = x.shape
    n_heads, rows, kvl = olat.shape
    vd = w["w_uvt"].shape[-1]
    ca = a.shape[-1]
    tm = _tile(rows, 512, 16)
    assert row0 % tm == 0
    blk0 = row0 // tm
    return pl.pallas_call(
        _oproj_kernel,
        grid=(rows // tm,),
        in_specs=[pl.BlockSpec((tm, D), lambda i: (blk0 + i, 0)),
                  pl.BlockSpec((tm, ca), lambda i: (i, 0)),
                  pl.BlockSpec((n_heads, tm, kvl), lambda i: (0, i, 0)),
                  _layer((n_heads, kvl, vd), l), _layer((ca, D), l), _layer((n_heads * vd, D), l)],
        out_specs=pl.BlockSpec((tm, D), lambda i: (blk0 + i, 0)),
        out_shape=jax.ShapeDtypeStruct((M, D), F32),
        input_output_aliases={0: 0},
        compiler_params=_params("parallel"),
        name="out_proj",
    )(x, a, olat, w["w_uvt"], w["w_o_a"], w["w_o_o"])


def _softmax_pv(s, v):
    m = jnp.max(s, axis=-1, keepdims=True)
    p = jnp.exp(s - m)
    return p, jnp.sum(p, axis=-1, keepdims=True)


def _xattn_prompt_kernel(x_ref, g_ref, wq_ref, mk_ref, mv_ref, wo_ref, out_ref, *, n_heads, scale):
    x = x_ref[...]
    h = _rms(x, g_ref[...]).astype(BF16)
    q = (_dot(h, wq_ref[...]) * scale).astype(BF16)
    hd = q.shape[-1] // n_heads
    outs = []
    for i in range(n_heads):
        cs = slice(i * hd, (i + 1) * hd)
        s = _dot_nt(q[:, cs], mk_ref[:, cs].astype(BF16))
        m = jnp.max(s, axis=-1, keepdims=True)
        p = jnp.exp(s - m)
        den = jnp.sum(p, axis=-1, keepdims=True)
        outs.append((_dot(p.astype(BF16), mv_ref[:, cs].astype(BF16)) / den).astype(BF16))
    out_ref[...] = x + _dot(jnp.concatenate(outs, axis=-1), wo_ref[...])


def _xattn_prompt(x, mk, mv, w, l, batch, seq, n_heads):
    M, D = x.shape
    E = mk.shape[-1]
    n_mem = mk.shape[1] // batch
    tm = _tile(seq, 512, 16)
    per = seq // tm
    kernel = functools.partial(_xattn_prompt_kernel, n_heads=n_heads, scale=(E // n_heads) ** -0.5)
    return pl.pallas_call(
        kernel,
        grid=(batch * per,),
        in_specs=[pl.BlockSpec((tm, D), lambda i: (i, 0)), _layer((1, D), l), _layer((D, E), l),
                  pl.BlockSpec((None, n_mem, E), lambda i: (l, i // per, 0)),
                  pl.BlockSpec((None, n_mem, E), lambda i: (l, i // per, 0)),
                  _layer((E, D), l)],
        out_specs=pl.BlockSpec((tm, D), lambda i: (i, 0)),
        out_shape=jax.ShapeDtypeStruct((M, D), F32),
        input_output_aliases={0: 0},
        compiler_params=_params("parallel"),
        name="xattn_prompt",
    )(x, w["ln_x_g"], w["xq_w"], mk, mv, w["xo_w"])


def _xattn_sample_kernel(x_ref, g_ref, wq_ref, mk_ref, mv_ref, wo_ref, out_ref, *, n_heads, scale):
    x = x_ref[...]
    bb, kv_rows, hd = mk_ref.shape
    T = x.shape[0] // bb
    h = _rms(x, g_ref[...]).astype(BF16)
    q = (_dot(h, wq_ref[...]) * scale).reshape(bb, T, n_heads * hd)
    qs = jnp.concatenate([q[:, :, i * hd:(i + 1) * hd] for i in range(n_heads)], axis=1)
    s = jnp.einsum("bqd,bkd->bqk", qs.astype(BF16), mk_ref[...].astype(BF16), preferred_element_type=F32)
    q_head = lax.broadcasted_iota(jnp.int32, s.shape, 1) // T
    k_head = lax.rem(lax.broadcasted_iota(jnp.int32, s.shape, 2), n_heads)
    s = jnp.where(q_head == k_head, s, NEG)
    m = jnp.max(s, axis=-1, keepdims=True)
    p = jnp.exp(s - m)
    den = jnp.sum(p, axis=-1, keepdims=True)
    o = jnp.einsum("bqk,bkd->bqd", p.astype(BF16), mv_ref[...].astype(BF16), preferred_element_type=F32) / den
    o = jnp.concatenate([o[:, i * T:(i + 1) * T, :] for i in range(n_heads)], axis=-1)
    out_ref[...] = x + _dot(o.reshape(bb * T, n_heads * hd).astype(BF16), wo_ref[...])


def _xattn_sample(x, mem_k, mem_v, w, l, row0, T, n_heads):
    M, D = x.shape
    _, Bd, kv_rows, hd = mem_k.shape
    E = n_heads * hd
    bb = _tile(Bd, 8, 1)
    tm = bb * T
    assert row0 % tm == 0 and tm % 8 == 0
    blk0 = row0 // tm
    kernel = functools.partial(_xattn_sample_kernel, n_heads=n_heads, scale=hd ** -0.5)
    return pl.pallas_call(
        kernel,
        grid=(Bd // bb,),
        in_specs=[pl.BlockSpec((tm, D), lambda i: (blk0 + i, 0)), _layer((1, D), l), _layer((D, E), l),
                  pl.BlockSpec((None, bb, kv_rows, hd), lambda i: (l, i, 0, 0)),
                  pl.BlockSpec((None, bb, kv_rows, hd), lambda i: (l, i, 0, 0)),
                  _layer((E, D), l)],
        out_specs=pl.BlockSpec((tm, D), lambda i: (blk0 + i, 0)),
        out_shape=jax.ShapeDtypeStruct((M, D), F32),
        input_output_aliases={0: 0},
        compiler_params=_params("parallel"),
        name="xattn_sample",
    )(x, w["ln_x_g"], w["xq_w"], mem_k, mem_v, w["xo_w"])


def _ffn_kernel(x_ref, g_ref, wg_ref, wu_ref, wd_ref, out_ref, h_ref, acc_ref):
    f = pl.program_id(1)

    @pl.when(f == 0)
    def _():
        h_ref[...] = _rms(x_ref[...], g_ref[...]).astype(BF16)
        acc_ref[...] = jnp.zeros_like(acc_ref)

    h = h_ref[...]
    gate = _dot(h, wg_ref[...])
    act = (gate * jax.nn.sigmoid(gate) * _dot(h, wu_ref[...])).astype(BF16)
    acc_ref[...] += _dot(act, wd_ref[...])

    @pl.when(f == pl.num_programs(1) - 1)
    def _():
        out_ref[...] = x_ref[...] + acc_ref[...]


def _ffn(x, w, l):
    M, D = x.shape
    Fd = w["ffn_w_gate"].shape[-1]
    tm = _tile(M, 768, 16)
    tf = _tile(Fd, 512, 128)
    return pl.pallas_call(
        _ffn_kernel,
        grid=(M // tm, Fd // tf),
        in_specs=[pl.BlockSpec((tm, D), lambda i, f: (i, 0)), _layer((1, D), l),
                  pl.BlockSpec((None, D, tf), lambda i, f: (l, 0, f)),
                  pl.BlockSpec((None, D, tf), lambda i, f: (l, 0, f)),
                  pl.BlockSpec((None, tf, D), lambda i, f: (l, f, 0))],
        out_specs=pl.BlockSpec((tm, D), lambda i, f: (i, 0)),
        out_shape=jax.ShapeDtypeStruct((M, D), F32),
        scratch_shapes=[pltpu.VMEM((tm, D), BF16), pltpu.VMEM((tm, D), F32)],
        input_output_aliases={0: 0},
        compiler_params=_params("parallel", "arbitrary"),
        name="ffn",
    )(x, w["ln_ffn_g"], w["ffn_w_gate"], w["ffn_w_up"], w["ffn_w_down"])


def _norm_kernel(x_ref, g_ref, o_ref):
    o_ref[...] = _rms(x_ref[...], g_ref[...])


def _final_norm(x, g, row0, rows):
    M, D = x.shape
    tm = _tile(rows, 512)
    assert row0 % tm == 0
    blk0 = row0 // tm
    return pl.pallas_call(
        _norm_kernel,
        grid=(rows // tm,),
        in_specs=[pl.BlockSpec((tm, D), lambda i: (blk0 + i, 0)), _full((1, D))],
        out_specs=pl.BlockSpec((tm, D), lambda i: (i, 0)),
        out_shape=jax.ShapeDtypeStruct((rows, D), F32),
        compiler_params=_params("parallel"),
        name="final_norm",
    )(x, g)


def _half_swap_cols(w, width):
    shp = w.shape
    g = w.reshape(shp[:-1] + (shp[-1] // width, 2, width // 2))
    return jnp.flip(g, axis=-2).reshape(shp)


def _rope_tables(pos, rope):
    half = rope // 2
    inv = jnp.power(ROPE_THETA, -jnp.arange(half, dtype=F32) / half)
    ang = pos.astype(F32)[:, None] * inv[None, :]
    cos, sin = jnp.cos(ang), jnp.sin(ang)
    return jnp.concatenate([cos, cos], axis=-1), jnp.concatenate([-sin, sin], axis=-1)


def kernel(x_prompt, x_sample, mem_prompt, cache_kv_latent, cache_k_rope, cache_mem_k, cache_mem_v, state_conv, page_table, ln_mix_g, w_in, q_norm_g, w_uq, kv_norm_g, w_uk, w_uv, conv_w, conv_b, conv_ln_g, conv_ln_b, w_o, ln_x_g, ln_mem_g, xq_w, xk_w, xv_w, xo_w, ln_ffn_g, ffn_w_gate, ffn_w_up, ffn_w_down, ln_f_g):
    Bp, S, D = x_prompt.shape
    Bd, T, _ = x_sample.shape
    L = w_in.shape[0]
    n_mem = mem_prompt.shape[1]
    page = cache_kv_latent.shape[2]
    kvl = cache_kv_latent.shape[3]
    rope = cache_k_rope.shape[3]
    n_heads, nope = w_uk.shape[2], w_uk.shape[3]
    vd = w_uv.shape[3]
    conv_ch = conv_w.shape[2]
    K = conv_w.shape[1]
    ql = q_norm_g.shape[1]
    x_heads, x_hd = cache_mem_k.shape[3], cache_mem_k.shape[4]
    E = x_heads * x_hd
    past_len = page_table.shape[1] * page
    Mp, Ms = Bp * S, Bd * T
    scale = float(nope + rope) ** -0.5

    vec = lambda g: g.reshape(L, 1, -1)
    w_uq_h = w_uq.reshape(L, ql, n_heads, nope + rope)
    w_qr = w_uq_h[..., nope:].reshape(L, ql, n_heads * rope)
    w_kr = w_in[:, :, 2 * conv_ch + ql + kvl:]
    w = {
        "ln_mix_g": vec(ln_mix_g), "q_norm_g": vec(q_norm_g), "kv_norm_g": vec(kv_norm_g),
        "w_val": w_in[:, :, :conv_ch].astype(BF16),
        "w_gate": w_in[:, :, conv_ch:2 * conv_ch].astype(BF16),
        "w_cq": w_in[:, :, 2 * conv_ch:2 * conv_ch + ql].astype(BF16),
        "w_ckv": w_in[:, :, 2 * conv_ch + ql:2 * conv_ch + ql + kvl].astype(BF16),
        "w_kr": w_kr.astype(BF16),
        "w_kr_sw": _half_swap_cols(w_kr, rope).astype(BF16),
        "w_qn": w_uq_h[..., :nope].reshape(L, ql, n_heads * nope).astype(BF16),
        "w_qr": w_qr.astype(BF16),
        "w_qr_sw": _half_swap_cols(w_qr, rope).astype(BF16),
        "w_qr_t": jnp.swapaxes(w_qr, 1, 2).astype(BF16),
        "w_qr_sw_t": jnp.swapaxes(_half_swap_cols(w_qr, rope), 1, 2).astype(BF16),
        "w_ukt": jnp.transpose(w_uk, (0, 2, 3, 1)).astype(BF16),
        "w_uk_h": jnp.transpose(w_uk, (0, 2, 1, 3)).astype(BF16),
        "w_uvt": jnp.transpose(w_uv, (0, 2, 1, 3)).astype(BF16),
        "conv_w": conv_w, "conv_b": vec(conv_b), "conv_ln_g": vec(conv_ln_g), "conv_ln_b": vec(conv_ln_b),
        "w_o_a": w_o[:, :conv_ch].astype(BF16), "w_o_o": w_o[:, conv_ch:].astype(BF16),
        "ln_x_g": vec(ln_x_g), "xq_w": xq_w.astype(BF16), "xo_w": xo_w.astype(BF16),
        "ln_ffn_g": vec(ln_ffn_g), "ffn_w_gate": ffn_w_gate.astype(BF16),
        "ffn_w_up": ffn_w_up.astype(BF16), "ffn_w_down": ffn_w_down.astype(BF16),
    }
    pos = jnp.concatenate([jnp.tile(jnp.arange(S, dtype=jnp.int32), Bp),
                           jnp.tile(past_len + jnp.arange(T, dtype=jnp.int32), Bd)])
    cos, sin = _rope_tables(pos, rope)
    cos_t, sin_t = cos.T, sin.T

    mk_all, mv_all = _mem_kv(mem_prompt.reshape(Bp * n_mem, D), vec(ln_mem_g),
                             xk_w.astype(BF16), xv_w.astype(BF16))
    mem_k = cache_mem_k.reshape(L, Bd, n_mem * x_heads, x_hd)
    mem_v = cache_mem_v.reshape(L, Bd, n_mem * x_heads, x_hd)
    cache_kr_t = jnp.swapaxes(cache_k_rope, 2, 3)
    state_t = jnp.swapaxes(state_conv, 1, 2)

    x = jnp.concatenate([x_prompt.reshape(Mp, D), x_sample.reshape(Ms, D)], axis=0)
    conv_p, conv_s, kvl_all, kr_all = [], [], [], []
    for l in range(L):
        u, ckv, kr, ckvb, krb, ckv_t, qlat, qrope, qlat_t, qrope_t = _in_proj(x, w, l, cos, sin, cos_t, sin_t, scale)
        a_p, tail_p = _conv_prompt(u, w, l, Bp, S)
        a_s, tail_s = _conv_sample(state_t, u[Mp:].reshape(Bd, T, conv_ch), w, l)
        o_p = _attn_prompt(qlat_t, qrope_t, ckvb, krb, ckv_t, Bp, S)
        o_s = _attn_sample(page_table, qlat[:, Mp:].astype(F32), qrope[:, Mp:].astype(F32), ckv[Mp:], kr[Mp:],
                           cache_kv_latent, cache_kr_t, l, T)
        x = _out_proj(x, a_p, o_p, w, l, 0)
        x = _out_proj(x, a_s.reshape(Ms, conv_ch), o_s, w, l, Mp)
        x = _xattn_prompt(x, mk_all, mv_all, w, l, Bp, S, x_heads)
        x = _xattn_sample(x, mem_k, mem_v, w, l, Mp, T, x_heads)
        x = _ffn(x, w, l)
        conv_p.append(tail_p)
        conv_s.append(tail_s)
        kvl_all.append(ckv)
        kr_all.append(kr)

    g_f = ln_f_g.reshape(1, D)
    y_prompt = _final_norm(x, g_f, 0, Mp).reshape(Bp, S, D)
    y_sample = _final_norm(x, g_f, Mp, Ms).reshape(Bd, T, D)
    kvl_all = jnp.stack(kvl_all)
    kr_all = jnp.stack(kr_all)
    return (y_prompt, y_sample,
            jnp.stack(conv_p),
            kvl_all[:, :Mp].reshape(L, Bp, S, kvl),
            kr_all[:, :Mp].reshape(L, Bp, S, rope),
            mk_all.reshape(L, Bp, n_mem, x_heads, x_hd),
            mv_all.reshape(L, Bp, n_mem, x_heads, x_hd),
            jnp.swapaxes(jnp.stack(conv_s), 1, 2),
            kvl_all[:, Mp:].reshape(L, Bd, T, kvl),
            kr_all[:, Mp:].reshape(L, Bd, T, rope))
```

```python
import functools
import math

import jax
import jax.numpy as jnp
from jax import lax
from jax.experimental import pallas as pl
from jax.experimental.pallas import tpu as pltpu

F32 = jnp.float32
BF16 = jnp.bfloat16
EPS = 1e-6
ROPE_THETA = 10000.0
VMEM_LIMIT = 56 * 1024 * 1024
NEG = -1e30


def _dot(a, b):
    return jnp.dot(a, b, preferred_element_type=F32)


def _dot_nt(a, b):
    return lax.dot_general(a, b, (((1,), (1,)), ((), ())), preferred_element_type=F32)


def _rms(x, g):
    ms = jnp.mean(x * x, axis=-1, keepdims=True)
    return (x * lax.rsqrt(ms + EPS)) * g


def _tile(n, pref, mult=8):
    t = min(n, pref)
    while t > mult and (n % t or t % mult):
        t -= 1
    assert n % t == 0 and (t % mult == 0 or t == n), (n, pref, mult)
    return t


def _params(*sem):
    return pltpu.CompilerParams(dimension_semantics=sem, vmem_limit_bytes=VMEM_LIMIT)


def _full(shape):
    zeros = (0,) * len(shape)
    return pl.BlockSpec(shape, lambda *_: zeros)


def _layer(shape, l):
    zeros = (0,) * len(shape)
    return pl.BlockSpec((None,) + tuple(shape), lambda *_: (l,) + zeros)


def _memkv_kernel(m_ref, g_ref, wk_ref, wv_ref, k_ref, v_ref):
    h = _rms(m_ref[...], g_ref[...]).astype(BF16)
    k_ref[...] = _dot(h, wk_ref[...])
    v_ref[...] = _dot(h, wv_ref[...])


def _mem_kv(mem, g, wk, wv):
    Mm, D = mem.shape
    L, _, E = wk.shape
    tm = _tile(Mm, 512)
    return pl.pallas_call(
        _memkv_kernel,
        grid=(L, Mm // tm),
        in_specs=[pl.BlockSpec((tm, D), lambda l, i: (i, 0)),
                  pl.BlockSpec((None, 1, D), lambda l, i: (l, 0, 0)),
                  pl.BlockSpec((None, D, E), lambda l, i: (l, 0, 0)),
                  pl.BlockSpec((None, D, E), lambda l, i: (l, 0, 0))],
        out_specs=[pl.BlockSpec((None, tm, E), lambda l, i: (l, i, 0))] * 2,
        out_shape=[jax.ShapeDtypeStruct((L, Mm, E), F32)] * 2,
        compiler_params=_params("parallel", "parallel"),
        name="mem_kv",
    )(mem, g, wk, wv)


ATTN_BLOCK = 256


def _inproj_kernel(x_ref, g_ref, wval_ref, wgate_ref, wcq_ref, wckv_ref, wkr_ref, wkrs_ref,
                   qg_ref, kvg_ref, wqn_ref, wqr_ref, wqrs_ref, wqrt_ref, wqrst_ref, wukt_ref, wuk_ref,
                   cos_ref, sin_ref, cost_ref, sint_ref,
                   u_ref, ckv_ref, kr_ref, ckvb_ref, krb_ref, ckvt_ref, qlat_ref, qrope_ref, qlt_ref, qrt_ref,
                   *, scale):
    n_heads, nope, _ = wukt_ref.shape
    rope = cos_ref.shape[-1]
    tk = ckvt_ref.shape[-1]
    h = _rms(x_ref[...], g_ref[...]).astype(BF16)
    u_ref[...] = _dot(h, wval_ref[...]) * jax.nn.sigmoid(_dot(h, wgate_ref[...]))
    cqn = _rms(_dot(h, wcq_ref[...]), qg_ref[...]).astype(BF16)
    ckv = _rms(_dot(h, wckv_ref[...]), kvg_ref[...])
    ckv_ref[...] = ckv
    ckvb_ref[...] = ckv.astype(BF16)
    ckv_t = ckv.T.astype(BF16)
    for j in range(ckvt_ref.shape[0]):
        ckvt_ref[j] = ckv_t[:, j * tk:(j + 1) * tk]
    cos = cos_ref[...]
    sin = sin_ref[...]
    kr = _dot(h, wkr_ref[...]) * cos + _dot(h, wkrs_ref[...]) * sin
    kr_ref[...] = kr
    krb_ref[...] = kr.astype(BF16)
    qn = _dot(cqn, wqn_ref[...])
    qr = _dot(cqn, wqr_ref[...])
    qrs = _dot(cqn, wqrs_ref[...])
    qr_t = _dot_nt(wqrt_ref[...], cqn)
    qrs_t = _dot_nt(wqrst_ref[...], cqn)
    cos_t = cost_ref[...]
    sin_t = sint_ref[...]
    for hd in range(n_heads):
        qn_h = qn[:, hd * nope:(hd + 1) * nope].astype(BF16)
        qlat_ref[hd] = (_dot(qn_h, wukt_ref[hd]) * scale).astype(BF16)
        qlt_ref[hd] = (_dot_nt(wuk_ref[hd], qn_h) * scale).astype(BF16)
        sl = slice(hd * rope, (hd + 1) * rope)
        qrope_ref[hd] = ((qr[:, sl] * cos + qrs[:, sl] * sin) * scale).astype(BF16)
        qrt_ref[hd] = ((qr_t[sl, :] * cos_t + qrs_t[sl, :] * sin_t) * scale).astype(BF16)


def _in_proj(x, w, l, cos, sin, cos_t, sin_t, scale):
    M, D = x.shape
    tm = _tile(M, 512, ATTN_BLOCK)
    n_heads, nope, kvl = w["w_ukt"].shape[1:]
    conv_ch = w["w_val"].shape[-1]
    ql = w["w_cq"].shape[-1]
    rope = cos.shape[-1]
    nc = tm // ATTN_BLOCK
    row = lambda c: pl.BlockSpec((tm, c), lambda i: (i, 0))
    col = lambda c: pl.BlockSpec((c, tm), lambda i: (0, i))
    heads = lambda c: pl.BlockSpec((n_heads, tm, c), lambda i: (0, i, 0))
    heads_t = lambda c: pl.BlockSpec((n_heads, c, tm), lambda i: (0, 0, i))
    return pl.pallas_call(
        functools.partial(_inproj_kernel, scale=scale),
        grid=(M // tm,),
        in_specs=[row(D), _layer((1, D), l),
                  _layer((D, conv_ch), l), _layer((D, conv_ch), l), _layer((D, ql), l),
                  _layer((D, kvl), l), _layer((D, rope), l), _layer((D, rope), l),
                  _layer((1, ql), l), _layer((1, kvl), l),
                  _layer((ql, n_heads * nope), l), _layer((ql, n_heads * rope), l),
                  _layer((ql, n_heads * rope), l), _layer((n_heads * rope, ql), l),
                  _layer((n_heads * rope, ql), l), _layer((n_heads, nope, kvl), l),
                  _layer((n_heads, kvl, nope), l),
                  row(rope), row(rope), col(rope), col(rope)],
        out_specs=[row(conv_ch), row(kvl), row(rope), row(kvl), row(rope),
                   pl.BlockSpec((nc, kvl, ATTN_BLOCK), lambda i: (i, 0, 0)),
                   heads(kvl), heads(rope), heads_t(kvl), heads_t(rope)],
        out_shape=[jax.ShapeDtypeStruct((M, conv_ch), F32),
                   jax.ShapeDtypeStruct((M, kvl), F32),
                   jax.ShapeDtypeStruct((M, rope), F32),
                   jax.ShapeDtypeStruct((M, kvl), BF16),
                   jax.ShapeDtypeStruct((M, rope), BF16),
                   jax.ShapeDtypeStruct((M // ATTN_BLOCK, kvl, ATTN_BLOCK), BF16),
                   jax.ShapeDtypeStruct((n_heads, M, kvl), BF16),
                   jax.ShapeDtypeStruct((n_heads, M, rope), BF16),
                   jax.ShapeDtypeStruct((n_heads, kvl, M), BF16),
                   jax.ShapeDtypeStruct((n_heads, rope, M), BF16)],
        compiler_params=_params("parallel"),
        name="in_proj",
    )(x, w["ln_mix_g"], w["w_val"], w["w_gate"], w["w_cq"], w["w_ckv"], w["w_kr"], w["w_kr_sw"],
      w["q_norm_g"], w["kv_norm_g"], w["w_qn"], w["w_qr"], w["w_qr_sw"], w["w_qr_t"], w["w_qr_sw_t"],
      w["w_ukt"], w["w_uk_h"], cos, sin, cos_t, sin_t)


CONV_ROWS = 64
LANES = 128
SUBLANES = 8


def _ln_swish(y, g, b):
    mu = jnp.mean(y, axis=-1, keepdims=True)
    d = y - mu
    var = jnp.mean(d * d, axis=-1, keepdims=True)
    z = d * lax.rsqrt(var + EPS) * g + b
    return z * jax.nn.sigmoid(z)


def _conv_prompt_kernel(u_ref, prev_ref, w_ref, b_ref, g_ref, lb_ref, a_ref, tail_ref, hist_ref, shift_ref, y_ref,
                        *, halo):
    t = pl.program_id(1)
    tt, ch = u_ref.shape
    K = w_ref.shape[0]
    off = halo - (K - 1)

    @pl.when(t == 0)
    def _():
        hist_ref[0:halo, :] = jnp.zeros((halo, ch), F32)

    @pl.when(t > 0)
    def _():
        hist_ref[0:halo, :] = prev_ref[...]

    hist_ref[halo:halo + tt, :] = u_ref[...]
    n_sh = halo + tt - SUBLANES
    for r in range(1, SUBLANES):
        for c0 in range(0, ch, LANES):
            shift_ref[r - 1, 0:n_sh, c0:c0 + LANES] = hist_ref[r:r + n_sh, c0:c0 + LANES]
    for r0 in range(0, tt, CONV_ROWS):
        for c0 in range(0, ch, LANES):
            cs = slice(c0, c0 + LANES)
            acc = jnp.zeros((CONV_ROWS, LANES), F32)
            for k in range(K):
                r = (off + k) % SUBLANES
                base = off + k - r + r0
                rows = slice(base, base + CONV_ROWS)
                tap = hist_ref[rows, cs] if r == 0 else shift_ref[r - 1, rows, cs]
                acc = acc + w_ref[k:k + 1, cs] * tap
            y_ref[r0:r0 + CONV_ROWS, cs] = acc
    y = y_ref[...] + b_ref[...]
    a_ref[...] = _ln_swish(y, g_ref[...], lb_ref[...]).astype(BF16)
    tail_ref[...] = hist_ref[halo + tt - (K - 1):halo + tt, :]


def _conv_prompt(u, w, l, batch, seq):
    ch = u.shape[-1]
    K = w["conv_w"].shape[1]
    halo = 32
    assert K - 1 <= halo
    tt = _tile(seq, 128, CONV_ROWS)
    nt = seq // tt
    assert tt % halo == 0 and tt % CONV_ROWS == 0 and ch % LANES == 0
    hb = tt // halo
    return pl.pallas_call(
        functools.partial(_conv_prompt_kernel, halo=halo),
        grid=(batch, nt),
        in_specs=[pl.BlockSpec((tt, ch), lambda b, t: (b * nt + t, 0)),
                  pl.BlockSpec((halo, ch), lambda b, t: (jnp.maximum((b * nt + t) * hb - 1, 0), 0)),
                  _layer((K, ch), l), _layer((1, ch), l), _layer((1, ch), l), _layer((1, ch), l)],
        out_specs=[pl.BlockSpec((tt, ch), lambda b, t: (b * nt + t, 0)),
                   pl.BlockSpec((None, K - 1, ch), lambda b, t: (b, 0, 0))],
        out_shape=[jax.ShapeDtypeStruct((batch * seq, ch), BF16),
                   jax.ShapeDtypeStruct((batch, K - 1, ch), F32)],
        scratch_shapes=[pltpu.VMEM((halo + tt, ch), F32), pltpu.VMEM((SUBLANES - 1, halo + tt, ch), F32),
                        pltpu.VMEM((tt, ch), F32)],
        compiler_params=_params("parallel", "arbitrary"),
        name="conv_prompt",
    )(u, u, w["conv_w"], w["conv_b"], w["conv_ln_g"], w["conv_ln_b"])


def _conv_sample_kernel(st_ref, u_ref, w_ref, b_ref, g_ref, lb_ref, a_ref, ns_ref, hist_ref):
    bb, T, ch = u_ref.shape
    K = w_ref.shape[0]
    hist_ref[0:K - 1] = st_ref[...]
    for t in range(T):
        hist_ref[K - 1 + t] = u_ref[:, t, :]
    for t in range(T):
        cols = []
        for c0 in range(0, ch, LANES):
            cs = slice(c0, c0 + LANES)
            acc = jnp.zeros((bb, LANES), F32)
            for k in range(K):
                acc = acc + w_ref[k:k + 1, cs] * hist_ref[t + k, :, cs]
            cols.append(acc)
        y = jnp.concatenate(cols, axis=-1) + b_ref[...]
        a_ref[:, t, :] = _ln_swish(y, g_ref[...], lb_ref[...])
    ns_ref[...] = hist_ref[T:T + K - 1]


def _conv_sample(state_t, u_s, w, l):
    Bd, T, ch = u_s.shape
    K = w["conv_w"].shape[1]
    bb = _tile(Bd, SUBLANES, SUBLANES)
    return pl.pallas_call(
        _conv_sample_kernel,
        grid=(Bd // bb,),
        in_specs=[pl.BlockSpec((None, K - 1, bb, ch), lambda i: (l, 0, i, 0)),
                  pl.BlockSpec((bb, T, ch), lambda i: (i, 0, 0)),
                  _layer((K, ch), l), _layer((1, ch), l), _layer((1, ch), l), _layer((1, ch), l)],
        out_specs=[pl.BlockSpec((bb, T, ch), lambda i: (i, 0, 0)),
                   pl.BlockSpec((K - 1, bb, ch), lambda i: (0, i, 0))],
        out_shape=[jax.ShapeDtypeStruct((Bd, T, ch), F32),
                   jax.ShapeDtypeStruct((K - 1, Bd, ch), F32)],
        scratch_shapes=[pltpu.VMEM((K - 1 + T, bb, ch), F32)],
        compiler_params=_params("parallel"),
        name="conv_sample",
    )(state_t, u_s, w["conv_w"], w["conv_b"], w["conv_ln_g"], w["conv_ln_b"])


def _attn_prompt_kernel(qlt_ref, qrt_ref, k_ref, r_ref, kt_ref, o_ref, m_ref, l_ref, acc_ref):
    qi = pl.program_id(1)
    n_heads, kvl, tq = qlt_ref.shape
    tk = kt_ref.shape[-1]
    m_ref[...] = jnp.full(m_ref.shape, NEG, F32)
    l_ref[...] = jnp.zeros(l_ref.shape, F32)
    acc_ref[...] = jnp.zeros(acc_ref.shape, F32)

    def run_chunks(chunks, masked):
        loaded = {}

        def operands(ci):
            if ci not in loaded:
                c = chunks[ci]
                ks = pl.ds(pl.multiple_of(c * tk, tk), tk)
                loaded[ci] = (k_ref[ks, :], r_ref[ks, :], kt_ref[c])
            return loaded[ci]

        def scores(ci, hd):
            kc, rc, _ = operands(ci)
            return _dot(kc, qlt_ref[hd]) + _dot(rc, qrt_ref[hd])

        def accumulate(ci, hd, alpha, p):
            acc_ref[hd] = alpha * acc_ref[hd] + _dot(operands(ci)[2], p)

        if masked:
            key = chunks[0] * tk + lax.broadcasted_iota(jnp.int32, (tk, tq), 0)
            visible = key <= qi * tq + lax.broadcasted_iota(jnp.int32, (tk, tq), 1)
        units = [(ci, hd) for ci in range(len(chunks)) for hd in range(n_heads)]
        queue = [scores(*u) for u in units[:2]]
        pending = None
        for i, (c, hd) in enumerate(units):
            st = queue.pop(0)
            if i + 2 < len(units):
                queue.append(scores(*units[i + 2]))
            if masked:
                st = jnp.where(visible, st, NEG)
            m_old = m_ref[hd]
            m_new = jnp.maximum(m_old, jnp.max(st, axis=0, keepdims=True))
            alpha = jnp.exp(m_old - m_new)
            p = jnp.exp(st - m_new)
            l_ref[hd] = alpha * l_ref[hd] + jnp.sum(p, axis=0, keepdims=True)
            m_ref[hd] = m_new
            if pending is not None:
                accumulate(*pending)
            pending = (c, hd, alpha, p.astype(BF16))
        accumulate(*pending)

    def pair(j, carry):
        run_chunks([2 * j, 2 * j + 1], False)
        return carry

    lax.fori_loop(0, qi // 2, pair, 0)

    @pl.when(lax.rem(qi, 2) == 1)
    def _():
        run_chunks([qi - 1], False)

    run_chunks([qi], True)
    for hd in range(n_heads):
        o_ref[hd] = (acc_ref[hd] / l_ref[hd]).T.astype(BF16)


def _attn_prompt(qlat_t, qrope_t, ckvb, krb, ckv_t, batch, seq):
    n_heads, kvl, _ = qlat_t.shape
    rope = qrope_t.shape[1]
    tk = ckv_t.shape[-1]
    tq = tk
    assert seq % tq == 0
    nq = seq // tq
    return pl.pallas_call(
        _attn_prompt_kernel,
        grid=(batch, nq),
        in_specs=[pl.BlockSpec((n_heads, kvl, tq), lambda b, q: (0, 0, b * nq + q)),
                  pl.BlockSpec((n_heads, rope, tq), lambda b, q: (0, 0, b * nq + q)),
                  pl.BlockSpec((seq, kvl), lambda b, q: (b, 0)),
                  pl.BlockSpec((seq, rope), lambda b, q: (b, 0)),
                  pl.BlockSpec((nq, kvl, tk), lambda b, q: (b, 0, 0))],
        out_specs=pl.BlockSpec((n_heads, tq, kvl), lambda b, q: (0, b * nq + q, 0)),
        out_shape=jax.ShapeDtypeStruct((n_heads, batch * seq, kvl), BF16),
        scratch_shapes=[pltpu.VMEM((n_heads, 1, tq), F32), pltpu.VMEM((n_heads, 1, tq), F32),
                        pltpu.VMEM((n_heads, kvl, tq), F32)],
        compiler_params=_params("parallel", "arbitrary"),
        name="attn_prompt",
    )(qlat_t, qrope_t, ckvb, krb, ckv_t)


def _attn_sample_kernel(pt_ref, ql_ref, qr_ref, kn_ref, rn_ref, ckv_hbm, krt_hbm, o_ref,
                        kbuf, rbuf, sem, *, layer, chunk_pages):
    b = pl.program_id(0)
    nb = pl.num_programs(0)
    n_pages, page, kvl = kbuf.shape[1:]
    rope = rbuf.shape[2]
    n_heads, T, _ = ql_ref.shape
    rows = n_heads * T
    slot = lax.rem(b, 2)

    def page_copies(bb, sl, p):
        pg = pt_ref[bb, p]
        return (pltpu.make_async_copy(ckv_hbm.at[layer, pg], kbuf.at[sl, p], sem.at[0, sl]),
                pltpu.make_async_copy(krt_hbm.at[layer, pg], rbuf.at[sl, p], sem.at[1, sl]))

    def start_fetch(bb, sl):
        for p in range(n_pages):
            for cp in page_copies(bb, sl, p):
                cp.start(priority=p % 2)

    @pl.when(b == 0)
    def _():
        start_fetch(b, slot)

    @pl.when(b + 1 < nb)
    def _():
        start_fetch(b + 1, 1 - slot)

    ql = ql_ref[...].reshape(rows, kvl).astype(BF16)
    qr = qr_ref[...].reshape(rows, rope).astype(BF16)
    kn = kn_ref[...].astype(BF16)
    s_new = _dot_nt(ql, kn) + _dot_nt(qr, rn_ref[...].astype(BF16))
    q_tok = lax.rem(lax.broadcasted_iota(jnp.int32, (rows, T), 0), T)
    s_new = jnp.where(lax.broadcasted_iota(jnp.int32, (rows, T), 1) <= q_tok, s_new, NEG)
    m = jnp.max(s_new, axis=-1, keepdims=True)
    p_new = jnp.exp(s_new - m)
    den = jnp.sum(p_new, axis=-1, keepdims=True)
    acc = _dot(p_new.astype(BF16), kn)

    for p in range(n_pages):
        for cp in page_copies(b, slot, p):
            cp.wait()

    ck = chunk_pages * page
    n_chunks = n_pages // chunk_pages

    def scores(c):
        pages = range(c * chunk_pages, (c + 1) * chunk_pages)
        kc = kbuf[slot, pages.start:pages.stop].reshape(ck, kvl).astype(BF16)
        s = _dot_nt(ql, kc) + jnp.concatenate(
            [_dot(qr, rbuf[slot, p].astype(BF16)) for p in pages], axis=-1)
        return s, kc

    queue = [scores(c) for c in range(min(2, n_chunks))]
    pending = None
    for c in range(n_chunks):
        s, kc = queue.pop(0)
        if c + 2 < n_chunks:
            queue.append(scores(c + 2))
        m_new = jnp.maximum(m, jnp.max(s, axis=-1, keepdims=True))
        alpha = jnp.exp(m - m_new)
        pr = jnp.exp(s - m_new)
        den = alpha * den + jnp.sum(pr, axis=-1, keepdims=True)
        m = m_new
        if pending is not None:
            acc = pending[0] * acc + _dot(pending[1], pending[2])
        pending = (alpha, pr.astype(BF16), kc)
    acc = pending[0] * acc + _dot(pending[1], pending[2])
    o_ref[...] = (acc / den).reshape(n_heads, T, kvl)


def _attn_sample(page_table, qlat, qrope, ckv, kr, cache_kv, cache_kr_t, l, T):
    Bd, n_pages = page_table.shape
    n_heads, _, kvl = qlat.shape
    rope = qrope.shape[-1]
    page = cache_kv.shape[2]
    chunk_pages = _tile(n_pages, 8, 1)
    kernel = functools.partial(_attn_sample_kernel, layer=l, chunk_pages=chunk_pages)
    return pl.pallas_call(
        kernel,
        grid_spec=pltpu.PrefetchScalarGridSpec(
            num_scalar_prefetch=1,
            grid=(Bd,),
            in_specs=[pl.BlockSpec((n_heads, T, kvl), lambda b, pt: (0, b, 0)),
                      pl.BlockSpec((n_heads, T, rope), lambda b, pt: (0, b, 0)),
                      pl.BlockSpec((T, kvl), lambda b, pt: (b, 0)),
                      pl.BlockSpec((T, rope), lambda b, pt: (b, 0)),
                      pl.BlockSpec(memory_space=pl.ANY),
                      pl.BlockSpec(memory_space=pl.ANY)],
            out_specs=pl.BlockSpec((n_heads, T, kvl), lambda b, pt: (0, b, 0)),
            scratch_shapes=[pltpu.VMEM((2, n_pages, page, kvl), F32),
                            pltpu.VMEM((2, n_pages, rope, page), F32),
                            pltpu.SemaphoreType.DMA((2, 2))]),
        out_shape=jax.ShapeDtypeStruct((n_heads, Bd * T, kvl), F32),
        compiler_params=_params("arbitrary"),
        name="attn_sample",
    )(page_table, qlat, qrope, ckv, kr, cache_kv, cache_kr_t)


def _oproj_kernel(x_ref, a_ref, ol_ref, wuv_ref, woa_ref, woo_ref, out_ref):
    n_heads = ol_ref.shape[0]
    o = jnp.concatenate([_dot(ol_ref[hd].astype(BF16), wuv_ref[hd]).astype(BF16) for hd in range(n_heads)],
                        axis=-1)
    out_ref[...] = x_ref[...] + _dot(a_ref[...].astype(BF16), woa_ref[...]) + _dot(o, woo_ref[...])


def _out_proj(x, a, olat, w, l, row0):
    M, D = x.shape
    n_heads, rows, kvl = olat.shape
    vd = w["w_uvt"].shape[-1]
    ca = a.shape[-1]
    tm = _tile(rows, 512, 16)
    assert row0 % tm == 0
    blk0 = row0 // tm
    return pl.pallas_call(
        _oproj_kernel,
        grid=(rows // tm,),
        in_specs=[pl.BlockSpec((tm, D), lambda i: (blk0 + i, 0)),
                  pl.BlockSpec((tm, ca), lambda i: (i, 0)),
                  pl.BlockSpec((n_heads, tm, kvl), lambda i: (0, i, 0)),
                  _layer((n_heads, kvl, vd), l), _layer((ca, D), l), _layer((n_heads * vd, D), l)],
        out_specs=pl.BlockSpec((tm, D), lambda i: (blk0 + i, 0)),
        out_shape=jax.ShapeDtypeStruct((M, D), F32),
        input_output_aliases={0: 0},
        compiler_params=_params("parallel"),
        name="out_proj",
    )(x, a, olat, w["w_uvt"], w["w_o_a"], w["w_o_o"])


def _softmax_pv(s, v):
    m = jnp.max(s, axis=-1, keepdims=True)
    p = jnp.exp(s - m)
    return p, jnp.sum(p, axis=-1, keepdims=True)


def _xattn_prompt_kernel(x_ref, g_ref, wq_ref, mk_ref, mv_ref, wo_ref, out_ref, *, n_heads, scale):
    x = x_ref[...]
    h = _rms(x, g_ref[...]).astype(BF16)
    q = (_dot(h, wq_ref[...]) * scale).astype(BF16)
    hd = q.shape[-1] // n_heads
    outs = []
    for i in range(n_heads):
        cs = slice(i * hd, (i + 1) * hd)
        s = _dot_nt(q[:, cs], mk_ref[:, cs].astype(BF16))
        m = jnp.max(s, axis=-1, keepdims=True)
        p = jnp.exp(s - m)
        den = jnp.sum(p, axis=-1, keepdims=True)
        outs.append((_dot(p.astype(BF16), mv_ref[:, cs].astype(BF16)) / den).astype(BF16))
    out_ref[...] = x + _dot(jnp.concatenate(outs, axis=-1), wo_ref[...])


def _xattn_prompt(x, mk, mv, w, l, batch, seq, n_heads):
    M, D = x.shape
    E = mk.shape[-1]
    n_mem = mk.shape[1] // batch
    tm = _tile(seq, 512, 16)
    per = seq // tm
    kernel = functools.partial(_xattn_prompt_kernel, n_heads=n_heads, scale=(E // n_heads) ** -0.5)
    return pl.pallas_call(
        kernel,
        grid=(batch * per,),
        in_specs=[pl.BlockSpec((tm, D), lambda i: (i, 0)), _layer((1, D), l), _layer((D, E), l),
                  pl.BlockSpec((None, n_mem, E), lambda i: (l, i // per, 0)),
                  pl.BlockSpec((None, n_mem, E), lambda i: (l, i // per, 0)),
                  _layer((E, D), l)],
        out_specs=pl.BlockSpec((tm, D), lambda i: (i, 0)),
        out_shape=jax.ShapeDtypeStruct((M, D), F32),
        input_output_aliases={0: 0},
        compiler_params=_params("parallel"),
        name="xattn_prompt",
    )(x, w["ln_x_g"], w["xq_w"], mk, mv, w["xo_w"])


def _xattn_sample_kernel(x_ref, g_ref, wq_ref, mk_ref, mv_ref, wo_ref, out_ref, *, n_heads, scale):
    x = x_ref[...]
    bb, kv_rows, hd = mk_ref.shape
    T = x.shape[0] // bb
    h = _rms(x, g_ref[...]).astype(BF16)
    q = (_dot(h, wq_ref[...]) * scale).reshape(bb, T, n_heads * hd)
    qs = jnp.concatenate([q[:, :, i * hd:(i + 1) * hd] for i in range(n_heads)], axis=1)
    s = jnp.einsum("bqd,bkd->bqk", qs.astype(BF16), mk_ref[...].astype(BF16), preferred_element_type=F32)
    q_head = lax.broadcasted_iota(jnp.int32, s.shape, 1) // T
    k_head = lax.rem(lax.broadcasted_iota(jnp.int32, s.shape, 2), n_heads)
    s = jnp.where(q_head == k_head, s, NEG)
    m = jnp.max(s, axis=-1, keepdims=True)
    p = jnp.exp(s - m)
    den = jnp.sum(p, axis=-1, keepdims=True)
    o = jnp.einsum("bqk,bkd->bqd", p.astype(BF16), mv_ref[...].astype(BF16), preferred_element_type=F32) / den
    o = jnp.concatenate([o[:, i * T:(i + 1) * T, :] for i in range(n_heads)], axis=-1)
    out_ref[...] = x + _dot(o.reshape(bb * T, n_heads * hd).astype(BF16), wo_ref[...])


def _xattn_sample(x, mem_k, mem_v, w, l, row0, T, n_heads):
    M, D = x.shape
    _, Bd, kv_rows, hd = mem_k.shape
    E = n_heads * hd
    bb = _tile(Bd, 8, 1)
    tm = bb * T
    assert row0 % tm == 0 and tm % 8 == 0
    blk0 = row0 // tm
    kernel = functools.partial(_xattn_sample_kernel, n_heads=n_heads, scale=hd ** -0.5)
    return pl.pallas_call(
        kernel,
        grid=(Bd // bb,),
        in_specs=[pl.BlockSpec((tm, D), lambda i: (blk0 + i, 0)), _layer((1, D), l), _layer((D, E), l),
                  pl.BlockSpec((None, bb, kv_rows, hd), lambda i: (l, i, 0, 0)),
                  pl.BlockSpec((None, bb, kv_rows, hd), lambda i: (l, i, 0, 0)),
                  _layer((E, D), l)],
        out_specs=pl.BlockSpec((tm, D), lambda i: (blk0 + i, 0)),
        out_shape=jax.ShapeDtypeStruct((M, D), F32),
        input_output_aliases={0: 0},
        compiler_params=_params("parallel"),
        name="xattn_sample",
    )(x, w["ln_x_g"], w["xq_w"], mem_k, mem_v, w["xo_w"])


def _ffn_kernel(x_ref, g_ref, wg_ref, wu_ref, wd_ref, out_ref, h_ref, acc_ref):
    f = pl.program_id(1)

    @pl.when(f == 0)
    def _():
        h_ref[...] = _rms(x_ref[...], g_ref[...]).astype(BF16)
        acc_ref[...] = jnp.zeros_like(acc_ref)

    h = h_ref[...]
    gate = _dot(h, wg_ref[...])
    act = (gate * jax.nn.sigmoid(gate) * _dot(h, wu_ref[...])).astype(BF16)
    acc_ref[...] += _dot(act, wd_ref[...])

    @pl.when(f == pl.num_programs(1) - 1)
    def _():
        out_ref[...] = x_ref[...] + acc_ref[...]


def _ffn(x, w, l):
    M, D = x.shape
    Fd = w["ffn_w_gate"].shape[-1]
    tm = _tile(M, 768, 16)
    tf = _tile(Fd, 512, 128)
    return pl.pallas_call(
        _ffn_kernel,
        grid=(M // tm, Fd // tf),
        in_specs=[pl.BlockSpec((tm, D), lambda i, f: (i, 0)), _layer((1, D), l),
                  pl.BlockSpec((None, D, tf), lambda i, f: (l, 0, f)),
                  pl.BlockSpec((None, D, tf), lambda i, f: (l, 0, f)),
                  pl.BlockSpec((None, tf, D), lambda i, f: (l, f, 0))],
        out_specs=pl.BlockSpec((tm, D), lambda i, f: (i, 0)),
        out_shape=jax.ShapeDtypeStruct((M, D), F32),
        scratch_shapes=[pltpu.VMEM((tm, D), BF16), pltpu.VMEM((tm, D), F32)],
        input_output_aliases={0: 0},
        compiler_params=_params("parallel", "arbitrary"),
        name="ffn",
    )(x, w["ln_ffn_g"], w["ffn_w_gate"], w["ffn_w_up"], w["ffn_w_down"])


def _norm_kernel(x_ref, g_ref, o_ref):
    o_ref[...] = _rms(x_ref[...], g_ref[...])


def _final_norm(x, g, row0, rows):
    M, D = x.shape
    tm = _tile(rows, 512)
    assert row0 % tm == 0
    blk0 = row0 // tm
    return pl.pallas_call(
        _norm_kernel,
        grid=(rows // tm,),
        in_specs=[pl.BlockSpec((tm, D), lambda i: (blk0 + i, 0)), _full((1, D))],
        out_specs=pl.BlockSpec((tm, D), lambda i: (i, 0)),
        out_shape=jax.ShapeDtypeStruct((rows, D), F32),
        compiler_params=_params("parallel"),
        name="final_norm",
    )(x, g)


def _half_swap_cols(w, width):
    shp = w.shape
    g = w.reshape(shp[:-1] + (shp[-1] // width, 2, width // 2))
    return jnp.flip(g, axis=-2).reshape(shp)


def _rope_tables(pos, rope):
    half = rope // 2
    inv = jnp.power(ROPE_THETA, -jnp.arange(half, dtype=F32) / half)
    ang = pos.astype(F32)[:, None] * inv[None, :]
    cos, sin = jnp.cos(ang), jnp.sin(ang)
    return jnp.concatenate([cos, cos], axis=-1), jnp.concatenate([-sin, sin], axis=-1)


def kernel(x_prompt, x_sample, mem_prompt, cache_kv_latent, cache_k_rope, cache_mem_k, cache_mem_v, state_conv, page_table, ln_mix_g, w_in, q_norm_g, w_uq, kv_norm_g, w_uk, w_uv, conv_w, conv_b, conv_ln_g, conv_ln_b, w_o, ln_x_g, ln_mem_g, xq_w, xk_w, xv_w, xo_w, ln_ffn_g, ffn_w_gate, ffn_w_up, ffn_w_down, ln_f_g):
    Bp, S, D = x_prompt.shape
    Bd, T, _ = x_sample.shape
    L = w_in.shape[0]
    n_mem = mem_prompt.shape[1]
    page = cache_kv_latent.shape[2]
    kvl = cache_kv_latent.shape[3]
    rope = cache_k_rope.shape[3]
    n_heads, nope = w_uk.shape[2], w_uk.shape[3]
    vd = w_uv.shape[3]
    conv_ch = conv_w.shape[2]
    K = conv_w.shape[1]
    ql = q_norm_g.shape[1]
    x_heads, x_hd = cache_mem_k.shape[3], cache_mem_k.shape[4]
    E = x_heads * x_hd
    past_len = page_table.shape[1] * page
    Mp, Ms = Bp * S, Bd * T
    scale = float(nope + rope) ** -0.5

    vec = lambda g: g.reshape(L, 1, -1)
    w_uq_h = w_uq.reshape(L, ql, n_heads, nope + rope)
    w_qr = w_uq_h[..., nope:].reshape(L, ql, n_heads * rope)
    w_kr = w_in[:, :, 2 * conv_ch + ql + kvl:]
    w = {
        "ln_mix_g": vec(ln_mix_g), "q_norm_g": vec(q_norm_g), "kv_norm_g": vec(kv_norm_g),
        "w_val": w_in[:, :, :conv_ch].astype(BF16),
        "w_gate": w_in[:, :, conv_ch:2 * conv_ch].astype(BF16),
        "w_cq": w_in[:, :, 2 * conv_ch:2 * conv_ch + ql].astype(BF16),
        "w_ckv": w_in[:, :, 2 * conv_ch + ql:2 * conv_ch + ql + kvl].astype(BF16),
        "w_kr": w_kr.astype(BF16),
        "w_kr_sw": _half_swap_cols(w_kr, rope).astype(BF16),
        "w_qn": w_uq_h[..., :nope].reshape(L, ql, n_heads * nope).astype(BF16),
        "w_qr": w_qr.astype(BF16),
        "w_qr_sw": _half_swap_cols(w_qr, rope).astype(BF16),
        "w_qr_t": jnp.swapaxes(w_qr, 1, 2).astype(BF16),
        "w_qr_sw_t": jnp.swapaxes(_half_swap_cols(w_qr, rope), 1, 2).astype(BF16),
        "w_ukt": jnp.transpose(w_uk, (0, 2, 3, 1)).astype(BF16),
        "w_uk_h": jnp.transpose(w_uk, (0, 2, 1, 3)).astype(BF16),
        "w_uvt": jnp.transpose(w_uv, (0, 2, 1, 3)).astype(BF16),
        "conv_w": conv_w, "conv_b": vec(conv_b), "conv_ln_g": vec(conv_ln_g), "conv_ln_b": vec(conv_ln_b),
        "w_o_a": w_o[:, :conv_ch].astype(BF16), "w_o_o": w_o[:, conv_ch:].astype(BF16),
        "ln_x_g": vec(ln_x_g), "xq_w": xq_w.astype(BF16), "xo_w": xo_w.astype(BF16),
        "ln_ffn_g": vec(ln_ffn_g), "ffn_w_gate": ffn_w_gate.astype(BF16),
        "ffn_w_up": ffn_w_up.astype(BF16), "ffn_w_down": ffn_w_down.astype(BF16),
    }
    pos = jnp.concatenate([jnp.tile(jnp.arange(S, dtype=jnp.int32), Bp),
                           jnp.tile(past_len + jnp.arange(T, dtype=jnp.int32), Bd)])
    cos, sin = _rope_tables(pos, rope)
    cos_t, sin_t = cos.T, sin.T

    mk_all, mv_all = _mem_kv(mem_prompt.reshape(Bp * n_mem, D), vec(ln_mem_g),
                             xk_w.astype(BF16), xv_w.astype(BF16))
    mem_k = cache_mem_k.reshape(L, Bd, n_mem * x_heads, x_hd)
    mem_v = cache_mem_v.reshape(L, Bd, n_mem * x_heads, x_hd)
    cache_kr_t = jnp.swapaxes(cache_k_rope, 2, 3)
    state_t = jnp.swapaxes(state_conv, 1, 2)

    x = jnp.concatenate([x_prompt.reshape(Mp, D), x_sample.reshape(Ms, D)], axis=0)
    conv_p, conv_s, kvl_all, kr_all = [], [], [], []
    for l in range(L):
        u, ckv, kr, ckvb, krb, ckv_t, qlat, qrope, qlat_t, qrope_t = _in_proj(x, w, l, cos, sin, cos_t, sin_t, scale)
        a_p, tail_p = _conv_prompt(u, w, l, Bp, S)
        a_s, tail_s = _conv_sample(state_t, u[Mp:].reshape(Bd, T, conv_ch), w, l)
        o_p = _attn_prompt(qlat_t, qrope_t, ckvb, krb, ckv_t, Bp, S)
        o_s = _attn_sample(page_table, qlat[:, Mp:].astype(F32), qrope[:, Mp:].astype(F32), ckv[Mp:], kr[Mp:],
                           cache_kv_latent, cache_kr_t, l, T)
        x = _out_proj(x, a_p, o_p, w, l, 0)
        x = _out_proj(x, a_s.reshape(Ms, conv_ch), o_s, w, l, Mp)
        x = _xattn_prompt(x, mk_all, mv_all, w, l, Bp, S, x_heads)
        x = _xattn_sample(x, mem_k, mem_v, w, l, Mp, T, x_heads)
        x = _ffn(x, w, l)
        conv_p.append(tail_p)
        conv_s.append(tail_s)
        kvl_all.append(ckv)
        kr_all.append(kr)

    g_f = ln_f_g.reshape(1, D)
    y_prompt = _final_norm(x, g_f, 0, Mp).reshape(Bp, S, D)
    y_sample = _final_norm(x, g_f, Mp, Ms).reshape(Bd, T, D)
    kvl_all = jnp.stack(kvl_all)
    kr_all = jnp.stack(kr_all)
    return (y_prompt, y_sample,
            jnp.stack(conv_p),
            kvl_all[:, :Mp].reshape(L, Bp, S, kvl),
            kr_all[:, :Mp].reshape(L, Bp, S, rope),
            mk_all.reshape(L, Bp, n_mem, x_heads, x_hd),
            mv_all.reshape(L, Bp, n_mem, x_heads, x_hd),
            jnp.swapaxes(jnp.stack(conv_s), 1, 2),
            kvl_all[:, Mp:].reshape(L, Bd, T, kvl),
            kr_all[:, Mp:].reshape(L, Bd, T, rope))
```
